```python
import math, functools
import jax, jax.numpy as jnp
from jax import lax
import numpy as np

D_MODEL = 2048
BATCH = 2
SEQ = 4096
DEPTH = 4
DEC_BATCH = 8
DEC_SEQ = 1
PAST_LEN = 16384
PAGE_SIZE = 128

HEAD_DIM = 64
H_RW = 16
H_FOX = 16
W_RW = H_RW * HEAD_DIM
W_FOX = H_FOX * HEAD_DIM
LORA_W = 64
LORA_A = 64
LORA_G = 128
RW_COLS = 3 * W_RW + LORA_W + LORA_A + LORA_G
FOX_COLS = 3 * W_FOX + H_FOX
GATE_COLS = 2 * D_MODEL
IN_COLS = RW_COLS + FOX_COLS + GATE_COLS
D_FF = 5504
Q_BLOCK = 128
N_MOD = 9
ALPHA = (2 * DEPTH) ** 0.25
BETA = (8 * DEPTH) ** -0.25
LN_EPS = 1e-5
GN_EPS = 64e-5

kernel_name = "rwkv7_fox_gated_hybrid_step"


def layer_norm(x, g, b):
    xf = x.astype(jnp.float32)
    mu = xf.mean(-1, keepdims=True)
    var = jnp.square(xf - mu).mean(-1, keepdims=True)
    return ((xf - mu) * lax.rsqrt(var + LN_EPS) * g + b).astype(x.dtype)


def swiglu(h, w_up, w_down):
    a, b = jnp.split(h @ w_up, 2, axis=-1)
    return (jax.nn.silu(a) * b) @ w_down


def rwkv_scan(S0, r, w, k, v, kk, a):
    def step(S, inp):
        r_t, w_t, k_t, v_t, kk_t, a_t = inp
        sa = jnp.einsum('bhvk,bhk->bhv', S, -kk_t)
        S = (S * w_t[:, :, None, :] + sa[..., None] * (kk_t * a_t)[:, :, None, :]
             + v_t[..., None] * k_t[:, :, None, :])
        return S, jnp.einsum('bhvk,bhk->bhv', S, r_t)
    xs = tuple(jnp.swapaxes(t.astype(jnp.float32), 0, 1) for t in (r, w, k, v, kk, a))
    S, o = lax.scan(step, S0.astype(jnp.float32), xs)
    return S, jnp.swapaxes(o, 0, 1)


def rwkv_branch(z_rw, shift0, S0, P, l):
    B, T, _ = z_rw.shape
    z_prev = jnp.concatenate([shift0[:, None].astype(z_rw.dtype), z_rw[:, :-1]], axis=1)
    zs = z_rw + (z_prev - z_rw) * P['rw_mu'][l]
    r, k, v, wd, ad, gd = jnp.split(
        zs, [W_RW, 2 * W_RW, 3 * W_RW, 3 * W_RW + LORA_W, 3 * W_RW + LORA_W + LORA_A], axis=-1)
    w_log = -jax.nn.softplus(-(P['rw_w0'][l] + jnp.tanh(wd) @ P['rw_w2'][l])) - 0.5
    decay = jnp.exp(-jnp.exp(w_log.astype(jnp.float32)))
    a = jax.nn.sigmoid(P['rw_a0'][l] + ad @ P['rw_a2'][l])
    g = jax.nn.sigmoid(gd) @ P['rw_g2'][l]
    hs = lambda t: t.reshape(B, T, H_RW, HEAD_DIM)
    kk = hs(k * P['rw_kk'][l]).astype(jnp.float32)
    kk = kk / jnp.maximum(jnp.sqrt(jnp.sum(kk * kk, -1, keepdims=True)), 1e-12)
    k = k * (1 + (a - 1) * P['rw_ka'][l])
    S, o = rwkv_scan(S0, hs(r), hs(decay), hs(k), hs(v), kk, hs(a))
    mu = o.mean(-1, keepdims=True)
    var = jnp.square(o - mu).mean(-1, keepdims=True)
    o = ((o - mu) * lax.rsqrt(var + GN_EPS)).reshape(B, T, W_RW) * P['rw_lnw'][l] + P['rw_lnb'][l]
    bonus = jnp.sum(hs(r) * hs(k) * P['rw_rk'][l], -1, keepdims=True) * hs(v)
    o = (o + bonus.reshape(B, T, W_RW)) * g
    return o.astype(z_rw.dtype), S.astype(S0.dtype), z_rw[:, -1]


def fox_prompt(q, k, v, logf):
    B, T = q.shape[:2]
    Fk = jnp.swapaxes(jnp.cumsum(logf, axis=1), 1, 2)
    kpos = jnp.arange(T)
    scale = HEAD_DIM ** -0.5

    def block(i):
        s0 = i * Q_BLOCK
        qi = lax.dynamic_slice_in_dim(q, s0, Q_BLOCK, axis=1)
        Fi = lax.dynamic_slice_in_dim(Fk, s0, Q_BLOCK, axis=2)
        s = (jnp.einsum('bqhd,bkhd->bhqk', qi, k).astype(jnp.float32) * scale
             + Fi[..., None] - Fk[:, :, None, :])
        qpos = s0 + jnp.arange(Q_BLOCK)
        s = jnp.where(kpos[None, :] <= qpos[:, None], s, -jnp.inf)
        p = jax.nn.softmax(s, axis=-1)
        return jnp.einsum('bhqk,bkhd->bqhd', p.astype(v.dtype), v)

    o = lax.map(block, jnp.arange(T // Q_BLOCK))
    return jnp.moveaxis(o, 0, 1).reshape(B, T, W_FOX)


def fox_sample(q, k, v, logf, kc, vc, lfc, page_table):
    Bd, Tn = q.shape[:2]
    kp = kc[page_table].reshape(Bd, -1, H_FOX, HEAD_DIM)
    vp = vc[page_table].reshape(Bd, -1, H_FOX, HEAD_DIM)
    lfp = lfc[page_table].reshape(Bd, -1, H_FOX)
    past = kp.shape[1]
    K = jnp.concatenate([kp.astype(k.dtype), k], axis=1)
    V = jnp.concatenate([vp.astype(v.dtype), v], axis=1)
    F = jnp.swapaxes(jnp.cumsum(jnp.concatenate([lfp.astype(jnp.float32), logf], axis=1), axis=1), 1, 2)
    Fq = F[:, :, past:]
    s = (jnp.einsum('bqhd,bkhd->bhqk', q, K).astype(jnp.float32) * HEAD_DIM ** -0.5
         + Fq[..., None] - F[:, :, None, :])
    qpos = past + jnp.arange(Tn)
    kpos = jnp.arange(past + Tn)
    s = jnp.where(kpos[None, :] <= qpos[:, None], s, -jnp.inf)
    p = jax.nn.softmax(s, axis=-1)
    return jnp.einsum('bhqk,bkhd->bqhd', p.astype(V.dtype), V).reshape(Bd, Tn, W_FOX)


def mixer(h, P, l, shift0, S0, attend):
    B, T, _ = h.shape
    z = h @ P['w_in'][l]
    z_rw, z_fox, z_gate = jnp.split(z, [RW_COLS, RW_COLS + FOX_COLS], axis=-1)
    o_rw, S, shift = rwkv_branch(z_rw, shift0, S0, P, l)
    q, k, v, zf = jnp.split(z_fox, [W_FOX, 2 * W_FOX, 3 * W_FOX], axis=-1)
    q = q.reshape(B, T, H_FOX, HEAD_DIM)
    k = k.reshape(B, T, H_FOX, HEAD_DIM)
    v = v.reshape(B, T, H_FOX, HEAD_DIM)
    logf = jax.nn.log_sigmoid(zf.astype(jnp.float32) + P['fox_bf'][l])
    o_fox = attend(q, k, v, logf)
    g_rw, g_fox = jnp.split(jax.nn.sigmoid(z_gate), 2, axis=-1)
    m = g_rw * (o_rw @ P['w_up_rw'][l]) + g_fox * (o_fox @ P['w_up_fox'][l])
    return m @ P['w_out'][l], (k, v, logf, S, shift)


def layer(x, c, P, l, shift0, S0, attend):
    B = x.shape[0]
    mods = (jax.nn.silu(c) @ P['w_ada'][l] + P['b_ada'][l]).reshape(B, N_MOD, 1, D_MODEL)
    sh, sc, gt = mods[:, 0::3], mods[:, 1::3], mods[:, 2::3]
    h = x * (1 + sc[:, 0]) + sh[:, 0]
    x = layer_norm(ALPHA * x + 0.5 * gt[:, 0] * swiglu(h, P['w_ffn_up'][l, 0], P['w_ffn_down'][l, 0]),
                   P['ln_g'][l, 0], P['ln_b'][l, 0])
    h = x * (1 + sc[:, 1]) + sh[:, 1]
    mix, st = mixer(h, P, l, shift0, S0, attend)
    x = layer_norm(ALPHA * x + gt[:, 1] * mix, P['ln_g'][l, 1], P['ln_b'][l, 1])
    h = x * (1 + sc[:, 2]) + sh[:, 2]
    x = layer_norm(ALPHA * x + 0.5 * gt[:, 2] * swiglu(h, P['w_ffn_up'][l, 1], P['w_ffn_down'][l, 1]),
                   P['ln_g'][l, 2], P['ln_b'][l, 2])
    return x, st


def setup_inputs(seed: int = 0) -> dict:
    key = jax.random.key(seed)
    ks = iter(jax.random.split(key, 48))
    nrm = lambda shape, s: jax.random.normal(next(ks), shape, jnp.float32) * s
    uni = lambda shape, lo, hi: jax.random.uniform(next(ks), shape, jnp.float32, lo, hi)
    n_pages = PAST_LEN // PAGE_SIZE
    n_used = DEC_BATCH * n_pages
    n_phys = n_used + n_used // 4
    D = D_MODEL
    inp = {}
    inp['x_prompt'] = nrm((BATCH, SEQ, D), 1.0)
    inp['x_sample'] = nrm((DEC_BATCH, DEC_SEQ, D), 1.0)
    inp['c_prompt'] = nrm((BATCH, D), 1.0)
    inp['c_sample'] = nrm((DEC_BATCH, D), 1.0)
    inp['cache_k'] = nrm((DEPTH, n_phys, PAGE_SIZE, H_FOX, HEAD_DIM), 1.0)
    inp['cache_v'] = nrm((DEPTH, n_phys, PAGE_SIZE, H_FOX, HEAD_DIM), 1.0)
    inp['cache_logf'] = jax.nn.log_sigmoid(3.5 + nrm((DEPTH, n_phys, PAGE_SIZE, H_FOX), 1.0))
    inp['state_rwkv'] = nrm((DEPTH, DEC_BATCH, H_RW, HEAD_DIM, HEAD_DIM), 1.0)
    inp['state_shift'] = nrm((DEPTH, DEC_BATCH, RW_COLS), 1.0)
    inp['page_table'] = jax.random.permutation(next(ks), n_phys)[:n_used].reshape(
        DEC_BATCH, n_pages).astype(jnp.int32)
    inp['w_ada'] = nrm((DEPTH, D, N_MOD * D), 0.5 * D ** -0.5)
    inp['b_ada'] = nrm((DEPTH, N_MOD * D), 0.01)
    inp['ln_g'] = 1.0 + nrm((DEPTH, 3, D), 0.02)
    inp['ln_b'] = nrm((DEPTH, 3, D), 0.02)
    inp['w_ffn_up'] = nrm((DEPTH, 2, D, 2 * D_FF), D ** -0.5)
    inp['w_ffn_down'] = nrm((DEPTH, 2, D_FF, D), BETA * D_FF ** -0.5)
    inp['w_in'] = nrm((DEPTH, D, IN_COLS), D ** -0.5)
    inp['rw_mu'] = uni((DEPTH, RW_COLS), 0.0, 1.0)
    inp['rw_w0'] = uni((DEPTH, W_RW), -6.0, 0.0)
    inp['rw_w2'] = nrm((DEPTH, LORA_W, W_RW), 0.1 * LORA_W ** -0.5)
    inp['rw_a0'] = nrm((DEPTH, W_RW), 0.1)
    inp['rw_a2'] = nrm((DEPTH, LORA_A, W_RW), 0.1 * LORA_A ** -0.5)
    inp['rw_g2'] = nrm((DEPTH, LORA_G, W_RW), LORA_G ** -0.5)
    inp['rw_kk'] = 0.85 + nrm((DEPTH, W_RW), 0.02)
    inp['rw_ka'] = 1.0 + nrm((DEPTH, W_RW), 0.02)
    inp['rw_rk'] = nrm((DEPTH, H_RW, HEAD_DIM), 0.1)
    inp['rw_lnw'] = 1.0 + nrm((DEPTH, W_RW), 0.02)
    inp['rw_lnb'] = nrm((DEPTH, W_RW), 0.02)
    inp['fox_bf'] = uni((DEPTH, H_FOX), 1.0, 6.0)
    inp['w_up_rw'] = nrm((DEPTH, W_RW, D), BETA * W_RW ** -0.5)
    inp['w_up_fox'] = nrm((DEPTH, W_FOX, D), BETA * W_FOX ** -0.5)
    inp['w_out'] = nrm((DEPTH, D, D), BETA * D ** -0.5)
    return inp


def reference(x_prompt, x_sample, c_prompt, c_sample, cache_k, cache_v, cache_logf, state_rwkv,
              state_shift, page_table, w_ada, b_ada, ln_g, ln_b, w_ffn_up, w_ffn_down, w_in, rw_mu,
              rw_w0, rw_w2, rw_a0, rw_a2, rw_g2, rw_kk, rw_ka, rw_rk, rw_lnw, rw_lnb, fox_bf,
              w_up_rw, w_up_fox, w_out):
    P = dict(w_ada=w_ada, b_ada=b_ada, ln_g=ln_g, ln_b=ln_b, w_ffn_up=w_ffn_up, w_ffn_down=w_ffn_down,
             w_in=w_in, rw_mu=rw_mu, rw_w0=rw_w0, rw_w2=rw_w2, rw_a0=rw_a0, rw_a2=rw_a2, rw_g2=rw_g2,
             rw_kk=rw_kk, rw_ka=rw_ka, rw_rk=rw_rk, rw_lnw=rw_lnw, rw_lnb=rw_lnb, fox_bf=fox_bf,
             w_up_rw=w_up_rw, w_up_fox=w_up_fox, w_out=w_out)
    B = x_prompt.shape[0]
    xp, xs = x_prompt, x_sample
    kp_l, vp_l, lfp_l, Sp_l, shp_l = [], [], [], [], []
    ks_l, vs_l, lfs_l, Ss_l, shs_l = [], [], [], [], []
    for l in range(DEPTH):
        shift0 = jnp.zeros((B, RW_COLS), x_prompt.dtype)
        S0 = jnp.zeros((B, H_RW, HEAD_DIM, HEAD_DIM), jnp.float32)
        xp, (k_, v_, lf_, S_, sh_) = layer(xp, c_prompt, P, l, shift0, S0, fox_prompt)
        kp_l.append(k_); vp_l.append(v_); lfp_l.append(lf_); Sp_l.append(S_); shp_l.append(sh_)
        attend_s = functools.partial(fox_sample, kc=cache_k[l], vc=cache_v[l], lfc=cache_logf[l],
                                     page_table=page_table)
        xs, (k_, v_, lf_, S_, sh_) = layer(xs, c_sample, P, l, state_shift[l], state_rwkv[l], attend_s)
        ks_l.append(k_); vs_l.append(v_); lfs_l.append(lf_); Ss_l.append(S_); shs_l.append(sh_)
    return (xp, xs,
            jnp.stack(kp_l), jnp.stack(vp_l), jnp.stack(lfp_l), jnp.stack(Sp_l), jnp.stack(shp_l),
            jnp.stack(ks_l), jnp.stack(vs_l), jnp.stack(lfs_l), jnp.stack(Ss_l), jnp.stack(shs_l))
```

```python
import functools
import math

import jax
import jax.numpy as jnp
from jax import lax
from jax.experimental import pallas as pl
from jax.experimental.pallas import tpu as pltpu

F32 = jnp.float32
BF16 = jnp.bfloat16

HEAD_DIM = 64
N_HEADS = 16
W_BR = N_HEADS * HEAD_DIM
LORA_W, LORA_A, LORA_G = 64, 64, 128
RW_COLS = 3 * W_BR + LORA_W + LORA_A + LORA_G
FOX_COLS = 3 * W_BR + N_HEADS
FOX_PAD = 3 * W_BR + 128
N_MOD = 9
LN_EPS = 1e-5
GN_EPS = 64e-5
LANE = 128
CHUNK = 64
SUB = 16
VMEM_LIMIT = 52 * 1024 * 1024
HI = lax.Precision.HIGHEST


def _tile(n, pref, step=8):
    if n <= pref:
        return n
    for t in range(pref - pref % step, step - 1, -step):
        if n % t == 0:
            return t
    raise ValueError(f"no tile for {n}")


def _cparams(sem):
    return pltpu.CompilerParams(dimension_semantics=sem, vmem_limit_bytes=VMEM_LIMIT)


def _sigmoid(x):
    return 1.0 / (1.0 + jnp.exp(-x))


def _silu(x):
    return x * _sigmoid(x)


def _log_sigmoid(x):
    return -(jnp.maximum(-x, 0.0) + jnp.log1p(jnp.exp(-jnp.abs(x))))


def _layer_norm(y, g, b):
    mu = jnp.mean(y, axis=-1, keepdims=True)
    d = y - mu
    var = jnp.mean(d * d, axis=-1, keepdims=True)
    return d * lax.rsqrt(var + LN_EPS) * g + b


def _dot(a, b):
    return jnp.dot(a, b, preferred_element_type=F32)


def _dot_hi(a, b):
    return jnp.dot(a, b, preferred_element_type=F32, precision=HI)


def _dot_nt_hi(a, b):
    return lax.dot_general(a, b, (((1,), (1,)), ((), ())), preferred_element_type=F32, precision=HI)


def _dot_tn_hi(a, b):
    return lax.dot_general(a, b, (((0,), (0,)), ((), ())), preferred_element_type=F32, precision=HI)


def _ada_kernel(c_ref, w_ref, b_ref, o_ref):
    h = _silu(c_ref[...]).astype(BF16)
    o_ref[0] = _dot(h, w_ref[0].astype(BF16)) + b_ref[0]


def _ada_mods(c_all, w_ada, b_ada):
    depth, d, nm = w_ada.shape
    r = c_all.shape[0]
    tn = _tile(nm, 1024, LANE)
    return pl.pallas_call(
        _ada_kernel,
        grid=(depth, nm // tn),
        in_specs=[pl.BlockSpec((r, d), lambda l, j: (0, 0)),
                  pl.BlockSpec((1, d, tn), lambda l, j: (l, 0, j)),
                  pl.BlockSpec((1, 1, tn), lambda l, j: (l, 0, j))],
        out_specs=pl.BlockSpec((1, r, tn), lambda l, j: (l, 0, j)),
        out_shape=jax.ShapeDtypeStruct((depth, r, nm), F32),
        compiler_params=_cparams(("arbitrary", "arbitrary")),
        name="ada_mods",
    )(c_all, w_ada, b_ada.reshape(depth, 1, nm))


def _mod_spec(rows_per_batch, tm, d):
    if rows_per_batch == 1:
        return pl.BlockSpec((1, tm, d), lambda i, *_: (0, 0, 0))
    tiles = rows_per_batch // tm
    return pl.BlockSpec((1, 1, d), lambda i, *_: (i // tiles, 0, 0))


def _mod_arrays(mods, idx, rows_per_batch):
    m = mods[:, idx]
    return m[None] if rows_per_batch == 1 else m[:, None]


def _ffn_kernel(x_ref, sh_ref, sc_ref, gt_ref, wa_ref, wb_ref, wd_ref, g_ref, b_ref, o_ref,
                h_scr, acc_scr, *, alpha):
    f = pl.program_id(1)

    @pl.when(f == 0)
    def _():
        h_scr[...] = (x_ref[...] * (1.0 + sc_ref[0]) + sh_ref[0]).astype(BF16)
        acc_scr[...] = jnp.zeros_like(acc_scr)

    h = h_scr[...]
    a = _dot(h, wa_ref[...])
    b = _dot(h, wb_ref[...])
    acc_scr[...] += _dot((_silu(a) * b).astype(BF16), wd_ref[...])

    @pl.when(f == pl.num_programs(1) - 1)
    def _():
        y = alpha * x_ref[...] + 0.5 * gt_ref[0] * acc_scr[...]
        o_ref[...] = _layer_norm(y, g_ref[...], b_ref[...])


def _ffn(x, sh, sc, gt, w_up, w_down, ln_g, ln_b, rows_per_batch, alpha, tm, tf):
    m, d = x.shape
    fp = w_down.shape[0]
    nf = fp // tf
    ms = _mod_spec(rows_per_batch, tm, d)
    return pl.pallas_call(
        functools.partial(_ffn_kernel, alpha=alpha),
        grid=(m // tm, nf),
        in_specs=[pl.BlockSpec((tm, d), lambda i, f: (i, 0)), ms, ms, ms,
                  pl.BlockSpec((d, tf), lambda i, f: (0, f)),
                  pl.BlockSpec((d, tf), lambda i, f: (0, nf + f)),
                  pl.BlockSpec((tf, d), lambda i, f: (f, 0)),
                  pl.BlockSpec((1, d), lambda i, f: (0, 0)),
                  pl.BlockSpec((1, d), lambda i, f: (0, 0))],
        out_specs=pl.BlockSpec((tm, d), lambda i, f: (i, 0)),
        out_shape=jax.ShapeDtypeStruct((m, d), F32),
        scratch_shapes=[pltpu.VMEM((tm, d), BF16), pltpu.VMEM((tm, d), F32)],
        compiler_params=_cparams(("arbitrary", "arbitrary")),
        name="ffn",
    )(x, sh, sc, gt, w_up, w_up, w_down, ln_g.reshape(1, d), ln_b.reshape(1, d))


def _proj_kernel(x_ref, sh_ref, sc_ref, w_ref, o_ref, h_scr, *, act):
    @pl.when(pl.program_id(1) == 0)
    def _():
        h_scr[...] = (x_ref[...] * (1.0 + sc_ref[0]) + sh_ref[0]).astype(BF16)

    z = _dot(h_scr[...], w_ref[...])
    o_ref[...] = _sigmoid(z) if act == "sigmoid" else z


def _proj(x, sh, sc, w, rows_per_batch, tm, tn, act=None):
    m, d = x.shape
    n = w.shape[1]
    ms = _mod_spec(rows_per_batch, tm, d)
    return pl.pallas_call(
        functools.partial(_proj_kernel, act=act),
        grid=(m // tm, n // tn),
        in_specs=[pl.BlockSpec((tm, d), lambda i, j: (i, 0)), ms, ms,
                  pl.BlockSpec((d, tn), lambda i, j: (0, j))],
        out_specs=pl.BlockSpec((tm, tn), lambda i, j: (i, j)),
        out_shape=jax.ShapeDtypeStruct((m, n), F32),
        scratch_shapes=[pltpu.VMEM((tm, d), BF16)],
        compiler_params=_cparams(("arbitrary", "arbitrary")),
        name="proj",
    )(x, sh, sc, w)


def _merge_kernel(x_ref, gt_ref, orw_ref, ofox_ref, grw_ref, gfox_ref, wur_ref, wuf_ref, wo_ref,
                  g_ref, b_ref, o_ref, *, alpha):
    m = (grw_ref[...] * _dot(orw_ref[...], wur_ref[...])
         + gfox_ref[...] * _dot(ofox_ref[...], wuf_ref[...]))
    y = alpha * x_ref[...] + gt_ref[0] * _dot(m.astype(BF16), wo_ref[...])
    o_ref[...] = _layer_norm(y, g_ref[...], b_ref[...])


def _merge(x, gt, o_rw, o_fox, gates, w_up_rw, w_up_fox, w_out, ln_g, ln_b, rows_per_batch, alpha, tm):
    m, d = x.shape
    wb = o_rw.shape[1]
    ms = _mod_spec(rows_per_batch, tm, d)
    const = lambda shape: pl.BlockSpec(shape, lambda i: (0, 0), pipeline_mode=pl.Buffered(1))
    return pl.pallas_call(
        functools.partial(_merge_kernel, alpha=alpha),
        grid=(m // tm,),
        in_specs=[pl.BlockSpec((tm, d), lambda i: (i, 0)), ms,
                  pl.BlockSpec((tm, wb), lambda i: (i, 0)),
                  pl.BlockSpec((tm, wb), lambda i: (i, 0)),
                  pl.BlockSpec((tm, d), lambda i: (i, 0)),
                  pl.BlockSpec((tm, d), lambda i: (i, 1)),
                  const((wb, d)), const((wb, d)), const((d, d)), const((1, d)), const((1, d))],
        out_specs=pl.BlockSpec((tm, d), lambda i: (i, 0)),
        out_shape=jax.ShapeDtypeStruct((m, d), F32),
        compiler_params=_cparams(("arbitrary",)),
        name="merge",
    )(x, gt, o_rw, o_fox, gates, gates, w_up_rw, w_up_fox, w_out, ln_g.reshape(1, d), ln_b.reshape(1, d))


def _rw_prep_math(zs, w0, w2, a0, a2, g2):
    r = zs[:, 0:W_BR]
    k = zs[:, W_BR:2 * W_BR]
    v = zs[:, 2 * W_BR:3 * W_BR]
    c0 = 3 * W_BR
    wd = zs[:, c0:c0 + LORA_W]
    ad = zs[:, c0 + LORA_W:c0 + LORA_W + LORA_A]
    gd = zs[:, c0 + LORA_W + LORA_A:c0 + LORA_W + LORA_A + LORA_G]
    w_log = _log_sigmoid(w0 + _dot(jnp.tanh(wd).astype(BF16), w2)) - 0.5
    log_decay = -jnp.exp(w_log)
    a = _sigmoid(a0 + _dot(ad.astype(BF16), a2))
    g = _dot(_sigmoid(gd).astype(BF16), g2)
    return r, k, v, log_decay, a, g


def _rw_prep_prompt_kernel(z_ref, mu_ref, w0_ref, w2_ref, a0_ref, a2_ref, g2_ref,
                           r_ref, k_ref, v_ref, lw_ref, a_ref, g_ref, carry_scr, *, tiles_per_batch):
    i = pl.program_id(0)
    z = z_ref[...]
    tm = z.shape[0]

    @pl.when(i % tiles_per_batch == 0)
    def _():
        carry_scr[...] = jnp.zeros_like(carry_scr)

    row = lax.broadcasted_iota(jnp.int32, z.shape, 0)
    z_prev = jnp.where(row == 0, carry_scr[...], pltpu.roll(z, 1, 0))
    carry_scr[...] = z[tm - 1:tm, :]
    zs = z + (z_prev - z) * mu_ref[...]
    outs = _rw_prep_math(zs, w0_ref[...], w2_ref[...], a0_ref[...], a2_ref[...], g2_ref[...])
    for ref, val in zip((r_ref, k_ref, v_ref, lw_ref, a_ref, g_ref), outs):
        for h in range(N_HEADS):
            ref[0, h] = val[:, h * HEAD_DIM:(h + 1) * HEAD_DIM]


def _rw_prep_prompt(z_rw, batch, mu, w0, w2, a0, a2, g2, tm):
    m, cols = z_rw.shape
    t = m // batch
    tiles = t // tm
    vec = lambda n: pl.BlockSpec((1, n), lambda i: (0, 0))
    mat = lambda a: pl.BlockSpec(a.shape, lambda i: (0, 0))
    hm = pl.BlockSpec((1, N_HEADS, tm, HEAD_DIM), lambda i: (i // tiles, 0, i % tiles, 0))
    hm_shape = jax.ShapeDtypeStruct((batch, N_HEADS, t, HEAD_DIM), F32)
    return pl.pallas_call(
        functools.partial(_rw_prep_prompt_kernel, tiles_per_batch=tiles),
        grid=(m // tm,),
        in_specs=[pl.BlockSpec((tm, cols), lambda i: (i, 0)), vec(cols), vec(W_BR), mat(w2),
                  vec(W_BR), mat(a2), mat(g2)],
        out_specs=[hm] * 6,
        out_shape=[hm_shape] * 6,
        scratch_shapes=[pltpu.VMEM((1, cols), F32)],
        compiler_params=_cparams(("arbitrary",)),
        name="rw_prep_prompt",
    )(z_rw, mu.reshape(1, cols), w0.reshape(1, W_BR), w2, a0.reshape(1, W_BR), a2, g2)


def _rw_prep_sample_kernel(z_ref, prev_ref, mu_ref, w0_ref, w2_ref, a0_ref, a2_ref, g2_ref,
                           r_ref, k_ref, v_ref, lw_ref, a_ref, g_ref):
    z = z_ref[...]
    zs = z + (prev_ref[...] - z) * mu_ref[...]
    outs = _rw_prep_math(zs, w0_ref[...], w2_ref[...], a0_ref[...], a2_ref[...], g2_ref[...])
    for ref, val in zip((r_ref, k_ref, v_ref, lw_ref, a_ref, g_ref), outs):
        ref[...] = val


def _rw_prep_sample(z_rw, shift0, mu, w0, w2, a0, a2, g2):
    rows, cols = z_rw.shape
    full = lambda a: pl.BlockSpec(a.shape, lambda: (0,) * a.ndim)
    args = (z_rw, shift0, mu.reshape(1, cols), w0.reshape(1, W_BR), w2, a0.reshape(1, W_BR), a2, g2)
    out = jax.ShapeDtypeStruct((rows, W_BR), F32)
    return pl.pallas_call(
        _rw_prep_sample_kernel,
        in_specs=[full(a) for a in args],
        out_specs=[pl.BlockSpec((rows, W_BR), lambda: (0, 0))] * 6,
        out_shape=[out] * 6,
        compiler_params=pltpu.CompilerParams(vmem_limit_bytes=VMEM_LIMIT),
        name="rw_prep_sample",
    )(*args)


def _group_norm_gate(o, r, k_mod, v, g, rk, lnw, lnb):
    mu = jnp.mean(o, axis=-1, keepdims=True)
    d = o - mu
    var = jnp.mean(d * d, axis=-1, keepdims=True)
    bonus = jnp.sum(r * k_mod * rk, axis=-1, keepdims=True) * v
    return (d * lax.rsqrt(var + GN_EPS) * lnw + lnb + bonus) * g


def _rwkv_chunk_kernel(r_ref, k_ref, v_ref, lw_ref, a_ref, g_ref, kkw_ref, ka_ref, rk_ref, lnw_ref,
                       lnb_ref, o_ref, s_ref, s_scr):
    tb = r_ref.shape[2]
    c = CHUNK

    @pl.when(pl.program_id(2) == 0)
    def _():
        s_scr[...] = jnp.zeros_like(s_scr)

    row = lax.broadcasted_iota(jnp.int32, (c, c), 0)
    col = lax.broadcasted_iota(jnp.int32, (c, c), 1)
    lower = col <= row
    strict = col < row
    same_sub = (row // SUB) == (col // SUB)
    tri = jnp.where(lower, 1.0, 0.0).astype(F32)
    eye = jnp.where(row == col, 1.0, 0.0).astype(F32)
    kkw, ka, rk, lnw, lnb = kkw_ref[0], ka_ref[0], rk_ref[0], lnw_ref[0], lnb_ref[0]

    def chunk(ci, carry):
        sl = pl.ds(pl.multiple_of(ci * c, c), c)
        r, k, v, lw, a, g = (ref[0, 0, sl, :] for ref in (r_ref, k_ref, v_ref, lw_ref, a_ref, g_ref))
        kk = k * kkw
        kk = kk / jnp.maximum(jnp.sqrt(jnp.sum(kk * kk, axis=-1, keepdims=True)), 1e-12)
        k_mod = k * (1.0 + (a - 1.0) * ka)
        al = -kk
        be = kk * a
        cum = _dot_hi(tri, lw)
        tot = cum[c - 1:c, :]
        w_inv = jnp.exp(-cum)
        w_end = jnp.exp(tot - cum)
        ab = al * jnp.exp(cum - lw)
        rb = r * jnp.exp(cum)
        bt = be * w_inv
        kt = k_mod * w_inv
        s0 = s_scr[...]
        l_mat = jnp.where(strict, _dot_nt_hi(ab, bt), 0.0)
        m_ak = jnp.where(strict, _dot_nt_hi(ab, kt), 0.0)
        m_rb = jnp.where(lower, _dot_nt_hi(rb, bt), 0.0)
        m_rk = jnp.where(lower, _dot_nt_hi(rb, kt), 0.0)
        rhs = _dot_nt_hi(ab, s0) + _dot_hi(m_ak, v)
        l_d = jnp.where(same_sub, l_mat, 0.0)
        l_o = l_mat - l_d
        p2 = _dot_hi(l_d, l_d)
        p4 = _dot_hi(p2, p2)
        p8 = _dot_hi(p4, p4)
        t_d = eye + l_d
        t_d = t_d + _dot_hi(p2, t_d)
        t_d = t_d + _dot_hi(p4, t_d)
        t_d = t_d + _dot_hi(p8, t_d)
        q = _dot_hi(t_d, l_o)
        q2 = _dot_hi(q, q)
        u = _dot_hi(t_d, rhs)
        u = u + _dot_hi(q2, u)
        u = u + _dot_hi(q, u)
        o = _dot_nt_hi(rb, s0) + _dot_hi(m_rb, u) + _dot_hi(m_rk, v)
        s_scr[...] = s0 * jnp.exp(tot) + _dot_tn_hi(u, be * w_end) + _dot_tn_hi(v, k_mod * w_end)
        o_ref[0, 0, sl, :] = _group_norm_gate(o, r, k_mod, v, g, rk, lnw, lnb)
        return carry

    lax.fori_loop(0, tb // c, chunk, 0)

    @pl.when(pl.program_id(2) == pl.num_programs(2) - 1)
    def _():
        s_ref[0, 0] = s_scr[...]


def _rwkv_prompt_scan(r, k, v, lw, a, g, kkw, ka, rk, lnw, lnb, tb):
    b, h, t, n = r.shape
    seq = pl.BlockSpec((1, 1, tb, n), lambda bi, hi, ti: (bi, hi, ti, 0))
    par = pl.BlockSpec((1, 1, n), lambda bi, hi, ti: (hi, 0, 0))
    hp = lambda p: p.reshape(h, 1, n)
    return pl.pallas_call(
        _rwkv_chunk_kernel,
        grid=(b, h, t // tb),
        in_specs=[seq] * 6 + [par] * 5,
        out_specs=[seq, pl.BlockSpec((1, 1, n, n), lambda bi, hi, ti: (bi, hi, 0, 0))],
        out_shape=[jax.ShapeDtypeStruct((b, h, t, n), F32), jax.ShapeDtypeStruct((b, h, n, n), F32)],
        scratch_shapes=[pltpu.VMEM((n, n), F32)],
        compiler_params=_cparams(("arbitrary", "arbitrary", "arbitrary")),
        name="rwkv_chunk",
    )(r, k, v, lw, a, g, hp(kkw), hp(ka), hp(rk), hp(lnw), hp(lnb))


def _rwkv_step_kernel(p_ref, r_ref, k_ref, lw_ref, a_ref, v_ref, g_ref, kkw_ref, ka_ref, rk_ref,
                      lnw_ref, lnb_ref, o_ref, pn_ref):
    p = p_ref[0]
    r, k, lw, a = r_ref[0], k_ref[0], lw_ref[0], a_ref[0]
    v, g = v_ref[0], g_ref[0]
    kk = k * kkw_ref[...]
    kk = kk / jnp.maximum(jnp.sqrt(jnp.sum(kk * kk, axis=1, keepdims=True)), 1e-12)
    k_mod = k * (1.0 + (a - 1.0) * ka_ref[...])
    sa = jnp.sum(p * (-kk), axis=1, keepdims=True)
    p_new = p * jnp.exp(lw) + (kk * a) * sa + k_mod * v
    pn_ref[0] = p_new
    o = jnp.sum(p_new * r, axis=1, keepdims=True)
    mu = jnp.mean(o, axis=-1, keepdims=True)
    d = o - mu
    var = jnp.mean(d * d, axis=-1, keepdims=True)
    bonus = jnp.sum(r * k_mod * rk_ref[...], axis=1, keepdims=True) * v
    o_ref[0] = (d * lax.rsqrt(var + GN_EPS) * lnw_ref[...] + lnb_ref[...] + bonus) * g


def _rwkv_sample_step(state, r, k, v, lw, a, g, kkw, ka, rk, lnw, lnb):
    rows = state.shape[0]
    h, n = N_HEADS, HEAD_DIM
    col = lambda x: x.reshape(rows, h, n, 1)
    rowv = lambda x: x.reshape(rows, h, 1, n)
    pcol = lambda x: x.reshape(h, n, 1)
    prow = lambda x: x.reshape(h, 1, n)
    cs = pl.BlockSpec((1, h, n, 1), lambda i: (i, 0, 0, 0))
    rs = pl.BlockSpec((1, h, 1, n), lambda i: (i, 0, 0, 0))
    ss = pl.BlockSpec((1, h, n, n), lambda i: (i, 0, 0, 0))
    pc = pl.BlockSpec((h, n, 1), lambda i: (0, 0, 0))
    pr = pl.BlockSpec((h, 1, n), lambda i: (0, 0, 0))
    o, p_new = pl.pallas_call(
        _rwkv_step_kernel,
        grid=(rows,),
        in_specs=[ss, cs, cs, cs, cs, rs, rs, pc, pc, pc, pr, pr],
        out_specs=[rs, ss],
        out_shape=[jax.ShapeDtypeStruct((rows, h, 1, n), F32), jax.ShapeDtypeStruct((rows, h, n, n), F32)],
        compiler_params=_cparams(("arbitrary",)),
        name="rwkv_step",
    )(jnp.swapaxes(state, 2, 3), col(r), col(k), col(lw), col(a), rowv(v), rowv(g),
      pcol(kkw), pcol(ka), pcol(rk), prow(lnw), prow(lnb))
    return o.reshape(rows, h * n), jnp.swapaxes(p_new, 2, 3)


def _forget_kernel(zf_ref, bf_ref, lf_ref, cum_ref, carry_scr, *, tiles_per_batch):
    @pl.when(pl.program_id(0) % tiles_per_batch == 0)
    def _():
        carry_scr[...] = jnp.zeros_like(carry_scr)

    lf = _log_sigmoid(zf_ref[...][:, :N_HEADS] + bf_ref[...])
    tm = lf.shape[0]
    row = lax.broadcasted_iota(jnp.int32, (tm, tm), 0)
    col = lax.broadcasted_iota(jnp.int32, (tm, tm), 1)
    cum = _dot_hi(jnp.where(col <= row, 1.0, 0.0).astype(F32), lf) + carry_scr[...]
    carry_scr[...] = cum[tm - 1:tm, :]
    lf_ref[...] = lf
    cum_ref[...] = cum


def _forget(z_fox, batch, bf, tm):
    m = z_fox.shape[0]
    tiles = (m // batch) // tm
    out = jax.ShapeDtypeStruct((m, N_HEADS), F32)
    return pl.pallas_call(
        functools.partial(_forget_kernel, tiles_per_batch=tiles),
        grid=(m // tm,),
        in_specs=[pl.BlockSpec((tm, LANE), lambda i: (i, 3 * W_BR // LANE)),
                  pl.BlockSpec((1, N_HEADS), lambda i: (0, 0))],
        out_specs=[pl.BlockSpec((tm, N_HEADS), lambda i: (i, 0))] * 2,
        out_shape=[out, out],
        scratch_shapes=[pltpu.VMEM((1, N_HEADS), F32)],
        compiler_params=_cparams(("arbitrary",)),
        name="forget",
    )(z_fox, bf.reshape(1, N_HEADS))


def _fox_kernel(q_ref, k_ref, v_ref, fq_ref, fk_ref, o_ref, m_scr, l_scr, acc_scr):
    i, j = pl.program_id(1), pl.program_id(2)
    tq, tk = q_ref.shape[0], k_ref.shape[0]

    @pl.when(j == 0)
    def _():
        m_scr[...] = jnp.full_like(m_scr, -jnp.inf)
        l_scr[...] = jnp.zeros_like(l_scr)
        acc_scr[...] = jnp.zeros_like(acc_scr)

    @pl.when(j <= i)
    def _():
        q = (q_ref[...] * HEAD_DIM ** -0.5).astype(BF16)
        k = k_ref[...].astype(BF16)
        v = v_ref[...].astype(BF16)
        fq, fk = fq_ref[0], fk_ref[0]
        qpos = i * tq + lax.broadcasted_iota(jnp.int32, (tq, tk), 0)
        kpos = j * tk + lax.broadcasted_iota(jnp.int32, (tq, tk), 1)
        causal = kpos <= qpos
        for h in range(N_HEADS):
            hs = slice(h * HEAD_DIM, (h + 1) * HEAD_DIM)
            s = lax.dot_general(q[:, hs], k[:, hs], (((1,), (1,)), ((), ())), preferred_element_type=F32)
            s = jnp.where(causal, s + (fq[:, h:h + 1] - fk[h:h + 1, :]), -jnp.inf)
            m_prev = m_scr[h]
            m_new = jnp.maximum(m_prev, jnp.max(s, axis=-1, keepdims=True))
            p = jnp.exp(s - m_new)
            corr = jnp.exp(m_prev - m_new)
            l_scr[h] = corr * l_scr[h] + jnp.sum(p, axis=-1, keepdims=True)
            acc_scr[:, hs] = corr * acc_scr[:, hs] + _dot(p.astype(BF16), v[:, hs])
            m_scr[h] = m_new

    @pl.when(j == i)
    def _():
        for h in range(N_HEADS):
            hs = slice(h * HEAD_DIM, (h + 1) * HEAD_DIM)
            o_ref[:, hs] = (acc_scr[:, hs] / l_scr[h]).astype(o_ref.dtype)


def _fox_prompt(z_fox, f_rows, f_cols, batch, tq):
    m = z_fox.shape[0]
    t = m // batch
    nq = t // tq
    return pl.pallas_call(
        _fox_kernel,
        grid=(batch, nq, nq),
        in_specs=[pl.BlockSpec((tq, W_BR), lambda b, i, j: (b * nq + i, 0)),
                  pl.BlockSpec((tq, W_BR), lambda b, i, j: (b * nq + jnp.minimum(j, i), 1)),
                  pl.BlockSpec((tq, W_BR), lambda b, i, j: (b * nq + jnp.minimum(j, i), 2)),
                  pl.BlockSpec((1, tq, N_HEADS), lambda b, i, j: (b, i, 0)),
                  pl.BlockSpec((1, N_HEADS, tq), lambda b, i, j: (b, 0, jnp.minimum(j, i)))],
        out_specs=pl.BlockSpec((tq, W_BR), lambda b, i, j: (b * nq + i, 0)),
        out_shape=jax.ShapeDtypeStruct((m, W_BR), BF16),
        scratch_shapes=[pltpu.VMEM((N_HEADS, tq, 1), F32), pltpu.VMEM((N_HEADS, tq, 1), F32),
                        pltpu.VMEM((tq, W_BR), F32)],
        compiler_params=_cparams(("arbitrary", "arbitrary", "arbitrary")),
        name="fox_prompt",
    )(z_fox, z_fox, z_fox, f_rows, f_cols)


def _fox_paged_kernel(pt_ref, qbd_ref, kn_ref, vn_ref, lfn_ref, k_ref, v_ref, lf_ref, o_ref,
                      m_scr, l_scr, acc_scr, suf_scr):
    del pt_ref
    p_idx = pl.program_id(1)
    ps = k_ref.shape[2]
    qbd = qbd_ref[0].astype(BF16)
    hrow = lax.broadcasted_iota(jnp.int32, (N_HEADS, W_BR), 0)
    hcol = lax.broadcasted_iota(jnp.int32, (N_HEADS, W_BR), 1) // HEAD_DIM
    expand = jnp.where(hrow == hcol, 1.0, 0.0).astype(F32)
    scale = HEAD_DIM ** -0.5

    @pl.when(p_idx == 0)
    def _():
        m_scr[...] = _dot(kn_ref[0].astype(BF16), qbd) * scale
        l_scr[...] = jnp.ones_like(l_scr)
        acc_scr[...] = vn_ref[0]
        suf_scr[...] = lfn_ref[0]

    lf = lf_ref[0, 0]
    row = lax.broadcasted_iota(jnp.int32, (ps, ps), 0)
    col = lax.broadcasted_iota(jnp.int32, (ps, ps), 1)
    after = _dot_hi(jnp.where(col > row, 1.0, 0.0).astype(F32), lf) + suf_scr[...]
    suf_scr[...] = suf_scr[...] + jnp.sum(lf, axis=0, keepdims=True)
    s = _dot(k_ref[0, 0].astype(BF16), qbd) * scale + after
    m_prev = m_scr[...]
    m_new = jnp.maximum(m_prev, jnp.max(s, axis=0, keepdims=True))
    p = jnp.exp(s - m_new)
    corr = jnp.exp(m_prev - m_new)
    l_scr[...] = corr * l_scr[...] + jnp.sum(p, axis=0, keepdims=True)
    p_exp = _dot(p.astype(BF16), expand.astype(BF16))
    pv = jnp.sum(p_exp * v_ref[0, 0], axis=0, keepdims=True)
    acc_scr[...] = _dot_hi(corr, expand) * acc_scr[...] + pv
    m_scr[...] = m_new

    @pl.when(p_idx == pl.num_programs(1) - 1)
    def _():
        o_ref[0] = (acc_scr[...] / _dot_hi(l_scr[...], expand)).astype(o_ref.dtype)


def _fox_paged(q, k_new, v_new, lf_new, cache_k, cache_v, cache_lf, layer, page_table):
    rows, n_pages = page_table.shape
    depth, n_phys, ps = cache_lf.shape[:3]
    ck = cache_k.reshape(depth, n_phys, ps, W_BR)
    cv = cache_v.reshape(depth, n_phys, ps, W_BR)
    eye = jnp.eye(N_HEADS, dtype=F32)
    qbd = (q.reshape(rows, N_HEADS, HEAD_DIM, 1) * eye[None, :, None, :]).reshape(rows, W_BR, N_HEADS)
    page = lambda b, p, pt: (layer, pt[b, n_pages - 1 - p], 0, 0)
    vec = lambda n: pl.BlockSpec((1, 1, n), lambda b, p, pt: (b, 0, 0))
    grid_spec = pltpu.PrefetchScalarGridSpec(
        num_scalar_prefetch=1,
        grid=(rows, n_pages),
        in_specs=[pl.BlockSpec((1, W_BR, N_HEADS), lambda b, p, pt: (b, 0, 0)),
                  vec(W_BR), vec(W_BR), vec(N_HEADS),
                  pl.BlockSpec((1, 1, ps, W_BR), page),
                  pl.BlockSpec((1, 1, ps, W_BR), page),
                  pl.BlockSpec((1, 1, ps, N_HEADS), page)],
        out_specs=vec(W_BR),
        scratch_shapes=[pltpu.VMEM((1, N_HEADS), F32), pltpu.VMEM((1, N_HEADS), F32),
                        pltpu.VMEM((1, W_BR), F32), pltpu.VMEM((1, N_HEADS), F32)],
    )
    out = pl.pallas_call(
        _fox_paged_kernel,
        grid_spec=grid_spec,
        out_shape=jax.ShapeDtypeStruct((rows, 1, W_BR), BF16),
        compiler_params=_cparams(("arbitrary", "arbitrary")),
        name="fox_paged",
    )(page_table, qbd, k_new.reshape(rows, 1, W_BR), v_new.reshape(rows, 1, W_BR),
      lf_new.reshape(rows, 1, N_HEADS), ck, cv, cache_lf)
    return out.reshape(rows, W_BR)


def _forget_sample_kernel(zf_ref, bf_ref, lf_ref):
    lf_ref[...] = _log_sigmoid(zf_ref[...][:, :N_HEADS] + bf_ref[...])


def _forget_sample(z_fox, bf):
    rows = z_fox.shape[0]
    return pl.pallas_call(
        _forget_sample_kernel,
        grid=(1,),
        in_specs=[pl.BlockSpec((rows, LANE), lambda i: (0, 3 * W_BR // LANE)),
                  pl.BlockSpec((1, N_HEADS), lambda i: (0, 0))],
        out_specs=pl.BlockSpec((rows, N_HEADS), lambda i: (0, 0)),
        out_shape=jax.ShapeDtypeStruct((rows, N_HEADS), F32),
        name="forget_sample",
    )(z_fox, bf.reshape(1, N_HEADS))


def _pad_cols(w, n):
    return jnp.pad(w, ((0, 0), (0, n - w.shape[1])))


def _prep_layer_weights(w_ffn_up, w_ffn_down, w_in, w_up_rw, w_up_fox, w_out, tf):
    d_ff = w_ffn_down.shape[1]
    fp = -(-d_ff // tf) * tf
    ups, downs = [], []
    for j in range(w_ffn_up.shape[0]):
        wa = _pad_cols(w_ffn_up[j][:, :d_ff], fp)
        wb = _pad_cols(w_ffn_up[j][:, d_ff:], fp)
        ups.append(jnp.concatenate([wa, wb], axis=1).astype(BF16))
        downs.append(jnp.pad(w_ffn_down[j], ((0, fp - d_ff), (0, 0))).astype(BF16))
    d = w_in.shape[0]
    w_rw = w_in[:, :RW_COLS].astype(BF16)
    w_fox = _pad_cols(w_in[:, RW_COLS:RW_COLS + FOX_COLS], FOX_PAD).astype(BF16)
    w_gate = w_in[:, RW_COLS + FOX_COLS:].astype(BF16)
    assert w_gate.shape[1] == 2 * d
    return dict(up=ups, down=downs, rw=w_rw, fox=w_fox, gate=w_gate, up_rw=w_up_rw.astype(BF16),
                up_fox=w_up_fox.astype(BF16), out=w_out.astype(BF16))


def _layer(x, mods, lw, P, l, rows_per_batch, alpha, mixer_fn):
    m, d = x.shape
    seq = m if rows_per_batch == 1 else rows_per_batch
    tm_ffn = _tile(seq, 512)
    tm = _tile(seq, 1024)
    tf = min(512, lw["down"][0].shape[0])
    mod = lambda i: _mod_arrays(mods, i, rows_per_batch)
    x = _ffn(x, mod(0), mod(1), mod(2), lw["up"][0], lw["down"][0], P["ln_g"][l, 0], P["ln_b"][l, 0],
             rows_per_batch, alpha, tm_ffn, tf)
    sh, sc = mod(3), mod(4)
    z_rw = _proj(x, sh, sc, lw["rw"], rows_per_batch, tm, RW_COLS // 2)
    z_fox = _proj(x, sh, sc, lw["fox"], rows_per_batch, tm, FOX_PAD // 5)
    gates = _proj(x, sh, sc, lw["gate"], rows_per_batch, tm, _tile(d, 1024, LANE), act="sigmoid")
    o_rw, o_fox, st = mixer_fn(z_rw, z_fox)
    x = _merge(x, mod(5), o_rw, o_fox, gates, lw["up_rw"], lw["up_fox"], lw["out"],
               P["ln_g"][l, 1], P["ln_b"][l, 1], rows_per_batch, alpha, _tile(seq, 256))
    x = _ffn(x, mod(6), mod(7), mod(8), lw["up"][1], lw["down"][1], P["ln_g"][l, 2], P["ln_b"][l, 2],
             rows_per_batch, alpha, tm_ffn, tf)
    return x, st


def _mixer_prompt(z_rw, z_fox, P, l, batch):
    m = z_rw.shape[0]
    t = m // batch
    heads = lambda a: a.reshape(batch, t, N_HEADS, HEAD_DIM)
    rw = _rw_prep_prompt(z_rw, batch, P["rw_mu"][l], P["rw_w0"][l], P["rw_w2"][l].astype(BF16),
                         P["rw_a0"][l], P["rw_a2"][l].astype(BF16), P["rw_g2"][l].astype(BF16), _tile(t, 256))
    o_hm, s_fin = _rwkv_prompt_scan(*rw, P["rw_kk"][l], P["rw_ka"][l], P["rw_rk"][l].reshape(-1),
                                    P["rw_lnw"][l], P["rw_lnb"][l], _tile(t, 512))
    o_rw = jnp.swapaxes(o_hm, 1, 2).reshape(m, W_BR).astype(BF16)
    lf, cum = _forget(z_fox, batch, P["fox_bf"][l], _tile(t, 256))
    f_rows = cum.reshape(batch, t, N_HEADS)
    o_fox = _fox_prompt(z_fox, f_rows, jnp.swapaxes(f_rows, 1, 2), batch, _tile(t, 512))
    shift = z_rw.reshape(batch, t, RW_COLS)[:, -1]
    st = (heads(z_fox[:, W_BR:2 * W_BR]), heads(z_fox[:, 2 * W_BR:3 * W_BR]),
          lf.reshape(batch, t, N_HEADS), s_fin, shift)
    return o_rw, o_fox, st


def _mixer_sample(z_rw, z_fox, P, l, shift0, state0, caches, page_table):
    rows = z_rw.shape[0]
    rw = _rw_prep_sample(z_rw, shift0, P["rw_mu"][l], P["rw_w0"][l], P["rw_w2"][l].astype(BF16),
                         P["rw_a0"][l], P["rw_a2"][l].astype(BF16), P["rw_g2"][l].astype(BF16))
    o_rw, s_new = _rwkv_sample_step(state0, *rw, P["rw_kk"][l], P["rw_ka"][l], P["rw_rk"][l].reshape(-1),
                                    P["rw_lnw"][l], P["rw_lnb"][l])
    lf = _forget_sample(z_fox, P["fox_bf"][l])
    k_new = z_fox[:, W_BR:2 * W_BR]
    v_new = z_fox[:, 2 * W_BR:3 * W_BR]
    o_fox = _fox_paged(z_fox[:, :W_BR], k_new, v_new, lf, *caches, l, page_table)
    heads = lambda a: a.reshape(rows, 1, N_HEADS, HEAD_DIM)
    st = (heads(k_new), heads(v_new), lf.reshape(rows, 1, N_HEADS), s_new, z_rw)
    return o_rw.astype(BF16), o_fox, st


def kernel(x_prompt, x_sample, c_prompt, c_sample, cache_k, cache_v, cache_logf, state_rwkv, state_shift, page_table, w_ada, b_ada, ln_g, ln_b, w_ffn_up, w_ffn_down, w_in, rw_mu, rw_w0, rw_w2, rw_a0, rw_a2, rw_g2, rw_kk, rw_ka, rw_rk, rw_lnw, rw_lnb, fox_bf, w_up_rw, w_up_fox, w_out):
    P = dict(ln_g=ln_g, ln_b=ln_b, rw_mu=rw_mu, rw_w0=rw_w0, rw_w2=rw_w2, rw_a0=rw_a0, rw_a2=rw_a2,
             rw_g2=rw_g2, rw_kk=rw_kk, rw_ka=rw_ka, rw_rk=rw_rk, rw_lnw=rw_lnw, rw_lnb=rw_lnb, fox_bf=fox_bf)
    depth = w_ada.shape[0]
    bp, t, d = x_prompt.shape
    bs, ts, _ = x_sample.shape
    assert ts == 1
    alpha = (2 * depth) ** 0.25
    rows = 16
    c_all = jnp.zeros((rows, d), F32).at[:bp].set(c_prompt).at[bp:bp + bs].set(c_sample)
    mods = _ada_mods(c_all, w_ada, b_ada).reshape(depth, rows, N_MOD, d)
    xp = x_prompt.reshape(bp * t, d)
    xs = x_sample.reshape(bs, d)
    outs_p, outs_s = [], []
    for l in range(depth):
        lw = _prep_layer_weights(w_ffn_up[l], w_ffn_down[l], w_in[l], w_up_rw[l], w_up_fox[l], w_out[l], 512)
        xp, st = _layer(xp, mods[l, :bp], lw, P, l, t, alpha,
                        functools.partial(_mixer_prompt, P=P, l=l, batch=bp))
        outs_p.append(st)
        xs, st = _layer(xs, mods[l, bp:bp + bs], lw, P, l, 1, alpha,
                        functools.partial(_mixer_sample, P=P, l=l, shift0=state_shift[l], state0=state_rwkv[l],
                                          caches=(cache_k, cache_v, cache_logf), page_table=page_table))
        outs_s.append(st)
    stack = lambda outs, i: jnp.stack([o[i] for o in outs])
    return (xp.reshape(bp, t, d), xs.reshape(bs, 1, d),
            *(stack(outs_p, i) for i in range(5)), *(stack(outs_s, i) for i in range(5)))
```

```python
import functools
import math

import jax
import jax.numpy as jnp
from jax import lax
from jax.experimental import pallas as pl
from jax.experimental.pallas import tpu as pltpu

F32 = jnp.float32
BF16 = jnp.bfloat16

HEAD_DIM = 64
N_HEADS = 16
W_BR = N_HEADS * HEAD_DIM
LORA_W, LORA_A, LORA_G = 64, 64, 128
RW_COLS = 3 * W_BR + LORA_W + LORA_A + LORA_G
FOX_COLS = 3 * W_BR + N_HEADS
FOX_PAD = 3 * W_BR + 128
N_MOD = 9
LN_EPS = 1e-5
GN_EPS = 64e-5
LANE = 128
CHUNK = 64
SUB = 16
VMEM_LIMIT = 52 * 1024 * 1024
HI = lax.Precision.HIGHEST


def _tile(n, pref, step=8):
    if n <= pref:
        return n
    for t in range(pref - pref % step, step - 1, -step):
        if n % t == 0:
            return t
    raise ValueError(f"no tile for {n}")


def _cparams(sem):
    return pltpu.CompilerParams(dimension_semantics=sem, vmem_limit_bytes=VMEM_LIMIT)


def _sigmoid(x):
    return 1.0 / (1.0 + jnp.exp(-x))


def _silu(x):
    return x * _sigmoid(x)


def _log_sigmoid(x):
    return -(jnp.maximum(-x, 0.0) + jnp.log1p(jnp.exp(-jnp.abs(x))))


def _layer_norm(y, g, b):
    mu = jnp.mean(y, axis=-1, keepdims=True)
    d = y - mu
    var = jnp.mean(d * d, axis=-1, keepdims=True)
    return d * lax.rsqrt(var + LN_EPS) * g + b


def _dot(a, b):
    return jnp.dot(a, b, preferred_element_type=F32)


def _dot_hi(a, b):
    return jnp.dot(a, b, preferred_element_type=F32, precision=HI)


def _ada_kernel(c_ref, w_ref, b_ref, o_ref):
    h = _silu(c_ref[...]).astype(BF16)
    o_ref[0] = _dot(h, w_ref[0].astype(BF16)) + b_ref[0]


def _ada_mods(c_all, w_ada, b_ada):
    depth, d, nm = w_ada.shape
    r = c_all.shape[0]
    tn = _tile(nm, 1024, LANE)
    return pl.pallas_call(
        _ada_kernel,
        grid=(depth, nm // tn),
        in_specs=[pl.BlockSpec((r, d), lambda l, j: (0, 0)),
                  pl.BlockSpec((1, d, tn), lambda l, j: (l, 0, j)),
                  pl.BlockSpec((1, 1, tn), lambda l, j: (l, 0, j))],
        out_specs=pl.BlockSpec((1, r, tn), lambda l, j: (l, 0, j)),
        out_shape=jax.ShapeDtypeStruct((depth, r, nm), F32),
        compiler_params=_cparams(("arbitrary", "arbitrary")),
        name="ada_mods",
    )(c_all, w_ada, b_ada.reshape(depth, 1, nm))


def _mod_spec(rows_per_batch, tm, d):
    if rows_per_batch == 1:
        return pl.BlockSpec((1, tm, d), lambda i, *_: (0, 0, 0))
    tiles = rows_per_batch // tm
    return pl.BlockSpec((1, 1, d), lambda i, *_: (i // tiles, 0, 0))


def _mod_arrays(mods, idx, rows_per_batch):
    m = mods[:, idx]
    return m[None] if rows_per_batch == 1 else m[:, None]


def _ffn_kernel(x_ref, sh_ref, sc_ref, gt_ref, wa_ref, wb_ref, wd_ref, g_ref, b_ref, o_ref,
                h_scr, acc_scr, *, alpha):
    f = pl.program_id(1)

    @pl.when(f == 0)
    def _():
        h_scr[...] = (x_ref[...] * (1.0 + sc_ref[0]) + sh_ref[0]).astype(BF16)
        acc_scr[...] = jnp.zeros_like(acc_scr)

    h = h_scr[...]
    a = _dot(h, wa_ref[...])
    b = _dot(h, wb_ref[...])
    acc_scr[...] += _dot((_silu(a) * b).astype(BF16), wd_ref[...])

    @pl.when(f == pl.num_programs(1) - 1)
    def _():
        y = alpha * x_ref[...] + 0.5 * gt_ref[0] * acc_scr[...]
        o_ref[...] = _layer_norm(y, g_ref[...], b_ref[...])


def _ffn(x, sh, sc, gt, w_up, w_down, ln_g, ln_b, rows_per_batch, alpha, tm, tf):
    m, d = x.shape
    fp = w_down.shape[0]
    nf = fp // tf
    ms = _mod_spec(rows_per_batch, tm, d)
    return pl.pallas_call(
        functools.partial(_ffn_kernel, alpha=alpha),
        grid=(m // tm, nf),
        in_specs=[pl.BlockSpec((tm, d), lambda i, f: (i, 0)), ms, ms, ms,
                  pl.BlockSpec((d, tf), lambda i, f: (0, f)),
                  pl.BlockSpec((d, tf), lambda i, f: (0, nf + f)),
                  pl.BlockSpec((tf, d), lambda i, f: (f, 0)),
                  pl.BlockSpec((1, d), lambda i, f: (0, 0)),
                  pl.BlockSpec((1, d), lambda i, f: (0, 0))],
        out_specs=pl.BlockSpec((tm, d), lambda i, f: (i, 0)),
        out_shape=jax.ShapeDtypeStruct((m, d), F32),
        scratch_shapes=[pltpu.VMEM((tm, d), BF16), pltpu.VMEM((tm, d), F32)],
        compiler_params=_cparams(("arbitrary", "arbitrary")),
        name="ffn",
    )(x, sh, sc, gt, w_up, w_up, w_down, ln_g.reshape(1, d), ln_b.reshape(1, d))


def _proj_kernel(x_ref, sh_ref, sc_ref, w_ref, o_ref, h_scr, *, act):
    @pl.when(pl.program_id(1) == 0)
    def _():
        h_scr[...] = (x_ref[...] * (1.0 + sc_ref[0]) + sh_ref[0]).astype(BF16)

    z = _dot(h_scr[...], w_ref[...])
    o_ref[...] = _sigmoid(z) if act == "sigmoid" else z


def _proj(x, sh, sc, w, rows_per_batch, tm, tn, act=None):
    m, d = x.shape
    n = w.shape[1]
    ms = _mod_spec(rows_per_batch, tm, d)
    return pl.pallas_call(
        functools.partial(_proj_kernel, act=act),
        grid=(m // tm, n // tn),
        in_specs=[pl.BlockSpec((tm, d), lambda i, j: (i, 0)), ms, ms,
                  pl.BlockSpec((d, tn), lambda i, j: (0, j))],
        out_specs=pl.BlockSpec((tm, tn), lambda i, j: (i, j)),
        out_shape=jax.ShapeDtypeStruct((m, n), F32),
        scratch_shapes=[pltpu.VMEM((tm, d), BF16)],
        compiler_params=_cparams(("arbitrary", "arbitrary")),
        name="proj",
    )(x, sh, sc, w)


def _merge_kernel(x_ref, gt_ref, orw_ref, ofox_ref, grw_ref, gfox_ref, wur_ref, wuf_ref, wo_ref,
                  g_ref, b_ref, o_ref, *, alpha):
    m = (grw_ref[...] * _dot(orw_ref[...], wur_ref[...])
         + gfox_ref[...] * _dot(ofox_ref[...], wuf_ref[...]))
    y = alpha * x_ref[...] + gt_ref[0] * _dot(m.astype(BF16), wo_ref[...])
    o_ref[...] = _layer_norm(y, g_ref[...], b_ref[...])


def _merge(x, gt, o_rw, o_fox, gates, w_up_rw, w_up_fox, w_out, ln_g, ln_b, rows_per_batch, alpha, tm):
    m, d = x.shape
    wb = o_rw.shape[1]
    ms = _mod_spec(rows_per_batch, tm, d)
    const = lambda shape: pl.BlockSpec(shape, lambda i: (0, 0), pipeline_mode=pl.Buffered(1))
    return pl.pallas_call(
        functools.partial(_merge_kernel, alpha=alpha),
        grid=(m // tm,),
        in_specs=[pl.BlockSpec((tm, d), lambda i: (i, 0)), ms,
                  pl.BlockSpec((tm, wb), lambda i: (i, 0)),
                  pl.BlockSpec((tm, wb), lambda i: (i, 0)),
                  pl.BlockSpec((tm, d), lambda i: (i, 0)),
                  pl.BlockSpec((tm, d), lambda i: (i, 1)),
                  const((wb, d)), const((wb, d)), const((d, d)), const((1, d)), const((1, d))],
        out_specs=pl.BlockSpec((tm, d), lambda i: (i, 0)),
        out_shape=jax.ShapeDtypeStruct((m, d), F32),
        compiler_params=_cparams(("arbitrary",)),
        name="merge",
    )(x, gt, o_rw, o_fox, gates, gates, w_up_rw, w_up_fox, w_out, ln_g.reshape(1, d), ln_b.reshape(1, d))


def _rw_prep_math(zs, w0, w2, a0, a2, g2):
    r = zs[:, 0:W_BR]
    k = zs[:, W_BR:2 * W_BR]
    v = zs[:, 2 * W_BR:3 * W_BR]
    c0 = 3 * W_BR
    wd = zs[:, c0:c0 + LORA_W]
    ad = zs[:, c0 + LORA_W:c0 + LORA_W + LORA_A]
    gd = zs[:, c0 + LORA_W + LORA_A:c0 + LORA_W + LORA_A + LORA_G]
    w_log = _log_sigmoid(w0 + _dot(jnp.tanh(wd).astype(BF16), w2)) - 0.5
    log_decay = -jnp.exp(w_log)
    a = _sigmoid(a0 + _dot(ad.astype(BF16), a2))
    g = _dot(_sigmoid(gd).astype(BF16), g2)
    return r, k, v, log_decay, a, g


def _rw_prep_prompt_kernel(z_ref, mu_ref, w0_ref, w2_ref, a0_ref, a2_ref, g2_ref,
                           r_ref, k_ref, v_ref, lw_ref, a_ref, g_ref, carry_scr, *, tiles_per_batch):
    i = pl.program_id(0)
    z = z_ref[...]
    tm = z.shape[0]

    @pl.when(i % tiles_per_batch == 0)
    def _():
        carry_scr[...] = jnp.zeros_like(carry_scr)

    row = lax.broadcasted_iota(jnp.int32, z.shape, 0)
    z_prev = jnp.where(row == 0, carry_scr[...], pltpu.roll(z, 1, 0))
    carry_scr[...] = z[tm - 1:tm, :]
    zs = z + (z_prev - z) * mu_ref[...]
    outs = _rw_prep_math(zs, w0_ref[...], w2_ref[...], a0_ref[...], a2_ref[...], g2_ref[...])
    for ref, val in zip((r_ref, k_ref, v_ref, lw_ref, a_ref, g_ref), outs):
        ref[...] = val


def _rw_prep_prompt(z_rw, batch, mu, w0, w2, a0, a2, g2, tm):
    m, cols = z_rw.shape
    t = m // batch
    tiles = t // tm
    vec = lambda n: pl.BlockSpec((1, n), lambda i: (0, 0))
    mat = lambda a: pl.BlockSpec(a.shape, lambda i: (0, 0))
    hm = pl.BlockSpec((tm, W_BR), lambda i: (i, 0))
    hm_shape = jax.ShapeDtypeStruct((m, W_BR), F32)
    return pl.pallas_call(
        functools.partial(_rw_prep_prompt_kernel, tiles_per_batch=tiles),
        grid=(m // tm,),
        in_specs=[pl.BlockSpec((tm, cols), lambda i: (i, 0)), vec(cols), vec(W_BR), mat(w2),
                  vec(W_BR), mat(a2), mat(g2)],
        out_specs=[hm] * 6,
        out_shape=[hm_shape] * 6,
        scratch_shapes=[pltpu.VMEM((1, cols), F32)],
        compiler_params=_cparams(("arbitrary",)),
        name="rw_prep_prompt",
    )(z_rw, mu.reshape(1, cols), w0.reshape(1, W_BR), w2, a0.reshape(1, W_BR), a2, g2)


def _rw_prep_sample_kernel(z_ref, prev_ref, mu_ref, w0_ref, w2_ref, a0_ref, a2_ref, g2_ref,
                           r_ref, k_ref, v_ref, lw_ref, a_ref, g_ref):
    z = z_ref[...]
    zs = z + (prev_ref[...] - z) * mu_ref[...]
    outs = _rw_prep_math(zs, w0_ref[...], w2_ref[...], a0_ref[...], a2_ref[...], g2_ref[...])
    for ref, val in zip((r_ref, k_ref, v_ref, lw_ref, a_ref, g_ref), outs):
        ref[...] = val


def _rw_prep_sample(z_rw, shift0, mu, w0, w2, a0, a2, g2):
    rows, cols = z_rw.shape
    full = lambda a: pl.BlockSpec(a.shape, lambda: (0,) * a.ndim)
    args = (z_rw, shift0, mu.reshape(1, cols), w0.reshape(1, W_BR), w2, a0.reshape(1, W_BR), a2, g2)
    out = jax.ShapeDtypeStruct((rows, W_BR), F32)
    return pl.pallas_call(
        _rw_prep_sample_kernel,
        in_specs=[full(a) for a in args],
        out_specs=[pl.BlockSpec((rows, W_BR), lambda: (0, 0))] * 6,
        out_shape=[out] * 6,
        compiler_params=pltpu.CompilerParams(vmem_limit_bytes=VMEM_LIMIT),
        name="rw_prep_sample",
    )(*args)


def _split_dot(x, w, passes):
    acc, rem = None, x
    for i in range(passes):
        piece = rem.astype(BF16)
        term = _dot(piece, w)
        acc = term if acc is None else acc + term
        if i + 1 < passes:
            rem = rem - piece.astype(F32)
    return acc


def _split_dot_rhs(w, x, passes):
    acc, rem = None, x
    for i in range(passes):
        piece = rem.astype(BF16)
        term = _dot(w, piece)
        acc = term if acc is None else acc + term
        if i + 1 < passes:
            rem = rem - piece.astype(F32)
    return acc


def _rwkv_chunk_kernel(r_ref, k_ref, v_ref, lw_ref, a_ref, g_ref, kkw_ref, ka_ref, rk_ref, lnw_ref,
                       lnb_ref, o_ref, s_ref, s_scr, *, pairs):
    tb = r_ref.shape[0]
    c, n, pw = CHUNK, HEAD_DIM, 2 * HEAD_DIM
    assert c == n

    @pl.when(pl.program_id(2) == 0)
    def _():
        s_scr[...] = jnp.zeros_like(s_scr)

    row = lax.broadcasted_iota(jnp.int32, (c, pw), 0)
    lane = lax.broadcasted_iota(jnp.int32, (c, pw), 1)
    first = lane < n
    col = jnp.where(first, lane, lane - n)
    lower = col <= row
    strict = col < row
    same_sub = (row // SUB) == (col // SUB)
    eye = jnp.where(row == col, 1.0, 0.0).astype(F32)
    tr = lax.broadcasted_iota(jnp.int32, (c, c), 0)
    tc = lax.broadcasted_iota(jnp.int32, (c, c), 1)
    tri = jnp.where(tc <= tr, 1.0, 0.0).astype(BF16)
    br = lax.broadcasted_iota(jnp.int32, (pw, pw), 0) // n
    bc = lax.broadcasted_iota(jnp.int32, (pw, pw), 1) // n
    same_head = br == bc
    ones_bd = jnp.where(same_head, 1.0, 0.0).astype(BF16)

    def expand(x):
        return jnp.concatenate([jnp.where(first, x, 0.0), jnp.where(first, 0.0, x)], axis=0).astype(BF16)

    def mm(x, y):
        return _dot(x.astype(BF16), expand(y))

    nt = lambda x, y: lax.dot_general(x, y, (((1,), (1,)), ((), ())), preferred_element_type=F32)

    def chunk(ci, carry):
        sl = pl.ds(pl.multiple_of(ci * c, c), c)
        ps_ = range(pairs)
        each = lambda f, *xs: [f(*args) for args in zip(*xs)]
        cols = [slice(p * pw, (p + 1) * pw) for p in ps_]
        r, k, v, lw, a, g = ([ref[sl, cs] for cs in cols] for ref in (r_ref, k_ref, v_ref, lw_ref, a_ref, g_ref))
        kkw, ka, rk, lnw, lnb = ([ref[:, cs] for cs in cols] for ref in (kkw_ref, ka_ref, rk_ref, lnw_ref, lnb_ref))
        s0 = [s_scr[p] for p in ps_]
        kk = each(lambda x, w: x * w, k, kkw)
        nrm = each(lambda x: _split_dot(x * x, ones_bd, 2), kk)
        kk = each(lambda x, n2: x / jnp.maximum(jnp.sqrt(n2), 1e-12), kk, nrm)
        k_mod = each(lambda x, ai, w: x * (1.0 + (ai - 1.0) * w), k, a, ka)
        be = each(lambda x, ai: x * ai, kk, a)
        cum = each(lambda x: _split_dot_rhs(tri, x, 3), lw)
        tot = each(lambda x: x[c - 1:c, :], cum)
        w_inv = each(lambda x: jnp.exp(-x), cum)
        w_end = each(lambda x, t: jnp.exp(t - x), cum, tot)
        ab = each(lambda x, cu, l: -x * jnp.exp(cu - l), kk, cum, lw)
        rb = each(lambda x, cu: x * jnp.exp(cu), r, cum)
        lhs = each(lambda x, y: jnp.concatenate([x, y], axis=0).astype(BF16), ab, rb)
        big = each(lambda x, b_, km, wi: nt(x, jnp.concatenate([expand(b_ * wi), expand(km * wi)], axis=0)),
                   lhs, be, k_mod, w_inv)
        from_s = each(lambda x, s: nt(x, s.astype(BF16)), lhs, s0)
        l_mat = each(lambda x: jnp.where(strict, x[:c, :pw], 0.0), big)
        m_rb = each(lambda x: jnp.where(lower, x[c:, :pw], 0.0), big)
        m_k = each(lambda x: jnp.concatenate([jnp.where(strict, x[:c, pw:], 0.0),
                                              jnp.where(lower, x[c:, pw:], 0.0)], axis=0).astype(BF16), big)
        from_v = each(lambda x, vi: _dot(x, expand(vi)), m_k, v)
        rhs = each(lambda x, y: x[:c] + y[:c], from_s, from_v)
        l_d = each(lambda x: jnp.where(same_sub, x, 0.0), l_mat)
        l_o = each(lambda x, y: x - y, l_mat, l_d)
        p2 = each(mm, l_d, l_d)
        p4 = each(mm, p2, p2)
        p8 = each(mm, p4, p4)
        t_d = each(lambda x: eye + x, l_d)
        t_d = each(lambda t, p_: t + mm(p_, t), t_d, p2)
        t_d = each(lambda t, p_: t + mm(p_, t), t_d, p4)
        t_d = each(lambda t, p_: t + mm(p_, t), t_d, p8)
        q = each(mm, t_d, l_o)
        u = each(mm, t_d, rhs)
        q2 = each(mm, q, q)
        u = each(lambda x, y: x + mm(y, x), u, q2)
        u = each(lambda x, y: x + mm(y, x), u, q)
        o = each(lambda x, y, m_, ui: x[c:] + y[c:] + mm(m_, ui), from_s, from_v, m_rb, u)
        upd = each(lambda ui, vi, b_, km, we: lax.dot_general(
            jnp.concatenate([ui, vi], axis=0).astype(BF16),
            jnp.concatenate([b_ * we, km * we], axis=0).astype(BF16),
            (((0,), (0,)), ((), ())), preferred_element_type=F32), u, v, be, k_mod, w_end)
        s_new = each(lambda s, t, x: s * jnp.exp(t) + jnp.where(same_head, x, 0.0), s0, tot, upd)
        mu = each(lambda x: _split_dot(x, ones_bd, 2) * (1.0 / n), o)
        d = each(lambda x, y: x - y, o, mu)
        var = each(lambda x: _split_dot(x * x, ones_bd, 2) * (1.0 / n), d)
        bonus = each(lambda ri, km, w, vi: _split_dot(ri * km * w, ones_bd, 2) * vi, r, k_mod, rk, v)
        out = each(lambda di, va, w, b_, bo, gi: (di * lax.rsqrt(va + GN_EPS) * w + b_ + bo) * gi,
                   d, var, lnw, lnb, bonus, g)
        for p in ps_:
            o_ref[sl, cols[p]] = out[p].astype(o_ref.dtype)
            s_scr[p] = s_new[p]
        return carry

    lax.fori_loop(0, tb // c, chunk, 0)

    @pl.when(pl.program_id(2) == pl.num_programs(2) - 1)
    def _():
        for p in range(pairs):
            s = s_scr[p]
            s_ref[0, 2 * p] = s[:n, :n]
            s_ref[0, 2 * p + 1] = s[n:, n:]


def _rwkv_prompt_scan(r, k, v, lw, a, g, kkw, ka, rk, lnw, lnb, batch, tb, pairs):
    m, w = r.shape
    t = m // batch
    nt_ = t // tb
    gw = pairs * 2 * HEAD_DIM
    seq = pl.BlockSpec((tb, gw), lambda bi, gi, ti: (bi * nt_ + ti, gi))
    par = pl.BlockSpec((1, gw), lambda bi, gi, ti: (0, gi))
    vec = lambda x: x.reshape(1, w)
    return pl.pallas_call(
        functools.partial(_rwkv_chunk_kernel, pairs=pairs),
        grid=(batch, w // gw, nt_),
        in_specs=[seq] * 6 + [par] * 5,
        out_specs=[seq, pl.BlockSpec((1, 2 * pairs, HEAD_DIM, HEAD_DIM), lambda bi, gi, ti: (bi, gi, 0, 0))],
        out_shape=[jax.ShapeDtypeStruct((m, w), BF16),
                   jax.ShapeDtypeStruct((batch, N_HEADS, HEAD_DIM, HEAD_DIM), F32)],
        scratch_shapes=[pltpu.VMEM((pairs, 2 * HEAD_DIM, 2 * HEAD_DIM), F32)],
        compiler_params=_cparams(("arbitrary", "arbitrary", "arbitrary")),
        name="rwkv_chunk",
    )(r, k, v, lw, a, g, vec(kkw), vec(ka), vec(rk), vec(lnw), vec(lnb))


def _rwkv_step_kernel(p_ref, r_ref, k_ref, lw_ref, a_ref, v_ref, g_ref, kkw_ref, ka_ref, rk_ref,
                      lnw_ref, lnb_ref, o_ref, pn_ref):
    p = p_ref[0]
    r, k, lw, a = r_ref[0], k_ref[0], lw_ref[0], a_ref[0]
    v, g = v_ref[0], g_ref[0]
    kk = k * kkw_ref[...]
    kk = kk / jnp.maximum(jnp.sqrt(jnp.sum(kk * kk, axis=1, keepdims=True)), 1e-12)
    k_mod = k * (1.0 + (a - 1.0) * ka_ref[...])
    sa = jnp.sum(p * (-kk), axis=1, keepdims=True)
    p_new = p * jnp.exp(lw) + (kk * a) * sa + k_mod * v
    pn_ref[0] = p_new
    o = jnp.sum(p_new * r, axis=1, keepdims=True)
    mu = jnp.mean(o, axis=-1, keepdims=True)
    d = o - mu
    var = jnp.mean(d * d, axis=-1, keepdims=True)
    bonus = jnp.sum(r * k_mod * rk_ref[...], axis=1, keepdims=True) * v
    o_ref[0] = (d * lax.rsqrt(var + GN_EPS) * lnw_ref[...] + lnb_ref[...] + bonus) * g


def _rwkv_sample_step(state, r, k, v, lw, a, g, kkw, ka, rk, lnw, lnb):
    rows = state.shape[0]
    h, n = N_HEADS, HEAD_DIM
    col = lambda x: x.reshape(rows, h, n, 1)
    rowv = lambda x: x.reshape(rows, h, 1, n)
    pcol = lambda x: x.reshape(h, n, 1)
    prow = lambda x: x.reshape(h, 1, n)
    cs = pl.BlockSpec((1, h, n, 1), lambda i: (i, 0, 0, 0))
    rs = pl.BlockSpec((1, h, 1, n), lambda i: (i, 0, 0, 0))
    ss = pl.BlockSpec((1, h, n, n), lambda i: (i, 0, 0, 0))
    pc = pl.BlockSpec((h, n, 1), lambda i: (0, 0, 0))
    pr = pl.BlockSpec((h, 1, n), lambda i: (0, 0, 0))
    o, p_new = pl.pallas_call(
        _rwkv_step_kernel,
        grid=(rows,),
        in_specs=[ss, cs, cs, cs, cs, rs, rs, pc, pc, pc, pr, pr],
        out_specs=[rs, ss],
        out_shape=[jax.ShapeDtypeStruct((rows, h, 1, n), F32), jax.ShapeDtypeStruct((rows, h, n, n), F32)],
        compiler_params=_cparams(("arbitrary",)),
        name="rwkv_step",
    )(jnp.swapaxes(state, 2, 3), col(r), col(k), col(lw), col(a), rowv(v), rowv(g),
      pcol(kkw), pcol(ka), pcol(rk), prow(lnw), prow(lnb))
    return o.reshape(rows, h * n), jnp.swapaxes(p_new, 2, 3)


def _forget_kernel(zf_ref, bf_ref, lf_ref, cum_ref, carry_scr, *, tiles_per_batch):
    @pl.when(pl.program_id(0) % tiles_per_batch == 0)
    def _():
        carry_scr[...] = jnp.zeros_like(carry_scr)

    lf = _log_sigmoid(zf_ref[...][:, :N_HEADS] + bf_ref[...])
    tm = lf.shape[0]
    row = lax.broadcasted_iota(jnp.int32, (tm, tm), 0)
    col = lax.broadcasted_iota(jnp.int32, (tm, tm), 1)
    cum = _dot_hi(jnp.where(col <= row, 1.0, 0.0).astype(F32), lf) + carry_scr[...]
    carry_scr[...] = cum[tm - 1:tm, :]
    lf_ref[...] = lf
    cum_ref[...] = cum


def _forget(z_fox, batch, bf, tm):
    m = z_fox.shape[0]
    tiles = (m // batch) // tm
    out = jax.ShapeDtypeStruct((m, N_HEADS), F32)
    return pl.pallas_call(
        functools.partial(_forget_kernel, tiles_per_batch=tiles),
        grid=(m // tm,),
        in_specs=[pl.BlockSpec((tm, LANE), lambda i: (i, 3 * W_BR // LANE)),
                  pl.BlockSpec((1, N_HEADS), lambda i: (0, 0))],
        out_specs=[pl.BlockSpec((tm, N_HEADS), lambda i: (i, 0))] * 2,
        out_shape=[out, out],
        scratch_shapes=[pltpu.VMEM((1, N_HEADS), F32)],
        compiler_params=_cparams(("arbitrary",)),
        name="forget",
    )(z_fox, bf.reshape(1, N_HEADS))


def _fox_kernel(q_ref, k_ref, v_ref, fq_ref, fk_ref, o_ref, m_scr, l_scr, acc_scr):
    i, j = pl.program_id(1), pl.program_id(2)
    tq, tk = q_ref.shape[0], k_ref.shape[0]

    @pl.when(j == 0)
    def _():
        m_scr[...] = jnp.full_like(m_scr, -jnp.inf)
        l_scr[...] = jnp.zeros_like(l_scr)
        acc_scr[...] = jnp.zeros_like(acc_scr)

    def accumulate(on_diagonal):
        q = (q_ref[...] * HEAD_DIM ** -0.5).astype(BF16)
        k = k_ref[...].astype(BF16)
        v = v_ref[...].astype(BF16)
        fq, fk = fq_ref[0], fk_ref[0]
        if on_diagonal:
            causal = (lax.broadcasted_iota(jnp.int32, (tq, tk), 1)
                      <= lax.broadcasted_iota(jnp.int32, (tq, tk), 0))
        for h in range(N_HEADS):
            hs = slice(h * HEAD_DIM, (h + 1) * HEAD_DIM)
            s = lax.dot_general(q[:, hs], k[:, hs], (((1,), (1,)), ((), ())), preferred_element_type=F32)
            s = s + (fq[:, h:h + 1] - fk[h:h + 1, :])
            if on_diagonal:
                s = jnp.where(causal, s, -jnp.inf)
            m_prev = m_scr[h]
            m_new = jnp.maximum(m_prev, jnp.max(s, axis=-1, keepdims=True))
            p = jnp.exp(s - m_new)
            corr = jnp.exp(m_prev - m_new)
            l_scr[h] = corr * l_scr[h] + jnp.sum(p, axis=-1, keepdims=True)
            acc_scr[:, hs] = corr * acc_scr[:, hs] + _dot(p.astype(BF16), v[:, hs])
            m_scr[h] = m_new

    @pl.when(j < i)
    def _():
        accumulate(False)

    @pl.when(j == i)
    def _():
        accumulate(True)
        for h in range(N_HEADS):
            hs = slice(h * HEAD_DIM, (h + 1) * HEAD_DIM)
            o_ref[:, hs] = (acc_scr[:, hs] / l_scr[h]).astype(o_ref.dtype)


def _fox_prompt(z_fox, f_rows, f_cols, batch, tq):
    m = z_fox.shape[0]
    t = m // batch
    nq = t // tq
    return pl.pallas_call(
        _fox_kernel,
        grid=(batch, nq, nq),
        in_specs=[pl.BlockSpec((tq, W_BR), lambda b, i, j: (b * nq + i, 0)),
                  pl.BlockSpec((tq, W_BR), lambda b, i, j: (b * nq + jnp.minimum(j, i), 1)),
                  pl.BlockSpec((tq, W_BR), lambda b, i, j: (b * nq + jnp.minimum(j, i), 2)),
                  pl.BlockSpec((1, tq, N_HEADS), lambda b, i, j: (b, i, 0)),
                  pl.BlockSpec((1, N_HEADS, tq), lambda b, i, j: (b, 0, jnp.minimum(j, i)))],
        out_specs=pl.BlockSpec((tq, W_BR), lambda b, i, j: (b * nq + i, 0)),
        out_shape=jax.ShapeDtypeStruct((m, W_BR), BF16),
        scratch_shapes=[pltpu.VMEM((N_HEADS, tq, 1), F32), pltpu.VMEM((N_HEADS, tq, 1), F32),
                        pltpu.VMEM((tq, W_BR), F32)],
        compiler_params=_cparams(("arbitrary", "arbitrary", "arbitrary")),
        name="fox_prompt",
    )(z_fox, z_fox, z_fox, f_rows, f_cols)


PAGES_PER_STEP = 4


def _fox_paged_kernel(pt_ref, q_ref, kn_ref, vn_ref, lfn_ref, rep_ref, *refs):
    del pt_ref
    npg = PAGES_PER_STEP
    k_refs, v_refs, lf_refs = refs[:npg], refs[npg:2 * npg], refs[2 * npg:3 * npg]
    o_ref, m_scr, l_scr, acc_scr, suf_scr = refs[3 * npg:]
    step = pl.program_id(1)
    ps = lf_refs[0].shape[2]
    nh = N_HEADS
    scale = HEAD_DIM ** -0.5
    q = q_ref[0]
    later = (lax.broadcasted_iota(jnp.int32, (ps, ps), 0) > lax.broadcasted_iota(jnp.int32, (ps, ps), 1))
    later = jnp.where(later, 1.0, 0.0).astype(BF16)
    pos = lax.broadcasted_iota(jnp.int32, (ps, 1, ps), 0)
    lane = lax.broadcasted_iota(jnp.int32, (ps, 1, ps), 2)
    own_lane = pos == lane
    own_head = (lax.broadcasted_iota(jnp.int32, (nh, ps * nh), 1) % nh
                == lax.broadcasted_iota(jnp.int32, (nh, ps * nh), 0))

    @pl.when(step == 0)
    def _():
        m_scr[...] = jnp.sum(q * kn_ref[0], axis=-1, keepdims=True) * scale
        l_scr[...] = jnp.ones_like(l_scr)
        acc_scr[...] = vn_ref[0]
        suf_scr[...] = lfn_ref[0]

    for i in range(npg):
        lft = lf_refs[i][0, 0].T
        after = _split_dot(lft, later, 3) + suf_scr[...]
        suf_scr[...] = suf_scr[...] + jnp.sum(lft, axis=-1, keepdims=True)
        qk = jnp.sum(k_refs[i][0, 0] * q[None], axis=-1, keepdims=True)
        s = jnp.sum(jnp.where(own_lane, qk, 0.0), axis=0) * scale + after
        m_prev = m_scr[...]
        m_new = jnp.maximum(m_prev, jnp.max(s, axis=-1, keepdims=True))
        p = jnp.exp(s - m_new)
        corr = jnp.exp(m_prev - m_new)
        l_scr[...] = corr * l_scr[...] + jnp.sum(p, axis=-1, keepdims=True)
        p_rows = jnp.where(own_head, _dot(p.astype(BF16), rep_ref[...]), 0.0).astype(BF16)
        acc_scr[...] = corr * acc_scr[...] + _dot(p_rows, v_refs[i][0, 0].astype(BF16))
        m_scr[...] = m_new

    @pl.when(step == pl.num_programs(1) - 1)
    def _():
        o_ref[0] = (acc_scr[...] / l_scr[...]).astype(o_ref.dtype)


def _fox_paged(q, k_new, v_new, lf_new, cache_k, cache_v, cache_lf, layer, page_table):
    rows, n_pages = page_table.shape
    depth, n_phys, ps = cache_lf.shape[:3]
    npg = PAGES_PER_STEP
    assert n_pages % npg == 0
    nh, hd = N_HEADS, HEAD_DIM
    cv = cache_v.reshape(depth, n_phys, ps * nh, hd)
    heads = lambda x: x.reshape(rows, nh, hd)
    rep = (jnp.arange(ps)[:, None] == jnp.arange(ps * nh)[None, :] // nh).astype(BF16)

    def page(i, nd):
        return lambda b, p, pt: (layer, pt[b, n_pages - 1 - (p * npg + i)]) + (0,) * nd

    vec = pl.BlockSpec((1, nh, hd), lambda b, p, pt: (b, 0, 0))
    grid_spec = pltpu.PrefetchScalarGridSpec(
        num_scalar_prefetch=1,
        grid=(rows, n_pages // npg),
        in_specs=([vec, vec, vec, pl.BlockSpec((1, nh, 1), lambda b, p, pt: (b, 0, 0)),
                   pl.BlockSpec((ps, ps * nh), lambda b, p, pt: (0, 0))]
                  + [pl.BlockSpec((1, 1, ps, nh, hd), page(i, 3)) for i in range(npg)]
                  + [pl.BlockSpec((1, 1, ps * nh, hd), page(i, 2)) for i in range(npg)]
                  + [pl.BlockSpec((1, 1, ps, nh), page(i, 2)) for i in range(npg)]),
        out_specs=vec,
        scratch_shapes=[pltpu.VMEM((nh, 1), F32), pltpu.VMEM((nh, 1), F32),
                        pltpu.VMEM((nh, hd), F32), pltpu.VMEM((nh, 1), F32)],
    )
    out = pl.pallas_call(
        _fox_paged_kernel,
        grid_spec=grid_spec,
        out_shape=jax.ShapeDtypeStruct((rows, nh, hd), BF16),
        compiler_params=_cparams(("arbitrary", "arbitrary")),
        name="fox_paged",
    )(page_table, heads(q), heads(k_new), heads(v_new), lf_new.reshape(rows, nh, 1), rep,
      *([cache_k] * npg), *([cv] * npg), *([cache_lf] * npg))
    return out.reshape(rows, nh * hd)


def _forget_sample_kernel(zf_ref, bf_ref, lf_ref):
    lf_ref[...] = _log_sigmoid(zf_ref[...][:, :N_HEADS] + bf_ref[...])


def _forget_sample(z_fox, bf):
    rows = z_fox.shape[0]
    return pl.pallas_call(
        _forget_sample_kernel,
        grid=(1,),
        in_specs=[pl.BlockSpec((rows, LANE), lambda i: (0, 3 * W_BR // LANE)),
                  pl.BlockSpec((1, N_HEADS), lambda i: (0, 0))],
        out_specs=pl.BlockSpec((rows, N_HEADS), lambda i: (0, 0)),
        out_shape=jax.ShapeDtypeStruct((rows, N_HEADS), F32),
        name="forget_sample",
    )(z_fox, bf.reshape(1, N_HEADS))


def _pad_cols(w, n):
    return jnp.pad(w, ((0, 0), (0, n - w.shape[1])))


def _prep_layer_weights(w_ffn_up, w_ffn_down, w_in, w_up_rw, w_up_fox, w_out, tf):
    d_ff = w_ffn_down.shape[1]
    fp = -(-d_ff // tf) * tf
    ups, downs = [], []
    for j in range(w_ffn_up.shape[0]):
        wa = _pad_cols(w_ffn_up[j][:, :d_ff], fp)
        wb = _pad_cols(w_ffn_up[j][:, d_ff:], fp)
        ups.append(jnp.concatenate([wa, wb], axis=1).astype(BF16))
        downs.append(jnp.pad(w_ffn_down[j], ((0, fp - d_ff), (0, 0))).astype(BF16))
    d = w_in.shape[0]
    w_rw = w_in[:, :RW_COLS].astype(BF16)
    w_fox = _pad_cols(w_in[:, RW_COLS:RW_COLS + FOX_COLS], FOX_PAD).astype(BF16)
    w_gate = w_in[:, RW_COLS + FOX_COLS:].astype(BF16)
    assert w_gate.shape[1] == 2 * d
    return dict(up=ups, down=downs, rw=w_rw, fox=w_fox, gate=w_gate, up_rw=w_up_rw.astype(BF16),
                up_fox=w_up_fox.astype(BF16), out=w_out.astype(BF16))


def _layer(x, mods, lw, P, l, rows_per_batch, alpha, mixer_fn):
    m, d = x.shape
    seq = m if rows_per_batch == 1 else rows_per_batch
    tm_ffn = _tile(seq, 512)
    tm = _tile(seq, 1024)
    tf = min(512, lw["down"][0].shape[0])
    mod = lambda i: _mod_arrays(mods, i, rows_per_batch)
    x = _ffn(x, mod(0), mod(1), mod(2), lw["up"][0], lw["down"][0], P["ln_g"][l, 0], P["ln_b"][l, 0],
             rows_per_batch, alpha, tm_ffn, tf)
    sh, sc = mod(3), mod(4)
    z_rw = _proj(x, sh, sc, lw["rw"], rows_per_batch, tm, RW_COLS // 2)
    z_fox = _proj(x, sh, sc, lw["fox"], rows_per_batch, tm, FOX_PAD // 5)
    gates = _proj(x, sh, sc, lw["gate"], rows_per_batch, tm, _tile(d, 1024, LANE), act="sigmoid")
    o_rw, o_fox, st = mixer_fn(z_rw, z_fox)
    x = _merge(x, mod(5), o_rw, o_fox, gates, lw["up_rw"], lw["up_fox"], lw["out"],
               P["ln_g"][l, 1], P["ln_b"][l, 1], rows_per_batch, alpha, _tile(seq, 256))
    x = _ffn(x, mod(6), mod(7), mod(8), lw["up"][1], lw["down"][1], P["ln_g"][l, 2], P["ln_b"][l, 2],
             rows_per_batch, alpha, tm_ffn, tf)
    return x, st


def _mixer_prompt(z_rw, z_fox, P, l, batch):
    m = z_rw.shape[0]
    t = m // batch
    heads = lambda a: a.reshape(batch, t, N_HEADS, HEAD_DIM)
    rw = _rw_prep_prompt(z_rw, batch, P["rw_mu"][l], P["rw_w0"][l], P["rw_w2"][l].astype(BF16),
                         P["rw_a0"][l], P["rw_a2"][l].astype(BF16), P["rw_g2"][l].astype(BF16), _tile(t, 256))
    o_rw, s_fin = _rwkv_prompt_scan(*rw, P["rw_kk"][l], P["rw_ka"][l], P["rw_rk"][l].reshape(-1),
                                    P["rw_lnw"][l], P["rw_lnb"][l], batch, _tile(t, 512, CHUNK), 8)
    lf, cum = _forget(z_fox, batch, P["fox_bf"][l], _tile(t, 256))
    f_rows = cum.reshape(batch, t, N_HEADS)
    o_fox = _fox_prompt(z_fox, f_rows, jnp.swapaxes(f_rows, 1, 2), batch, _tile(t, 512))
    shift = z_rw.reshape(batch, t, RW_COLS)[:, -1]
    st = (heads(z_fox[:, W_BR:2 * W_BR]), heads(z_fox[:, 2 * W_BR:3 * W_BR]),
          lf.reshape(batch, t, N_HEADS), s_fin, shift)
    return o_rw, o_fox, st


def _mixer_sample(z_rw, z_fox, P, l, shift0, state0, caches, page_table):
    rows = z_rw.shape[0]
    rw = _rw_prep_sample(z_rw, shift0, P["rw_mu"][l], P["rw_w0"][l], P["rw_w2"][l].astype(BF16),
                         P["rw_a0"][l], P["rw_a2"][l].astype(BF16), P["rw_g2"][l].astype(BF16))
    o_rw, s_new = _rwkv_sample_step(state0, *rw, P["rw_kk"][l], P["rw_ka"][l], P["rw_rk"][l].reshape(-1),
                                    P["rw_lnw"][l], P["rw_lnb"][l])
    lf = _forget_sample(z_fox, P["fox_bf"][l])
    k_new = z_fox[:, W_BR:2 * W_BR]
    v_new = z_fox[:, 2 * W_BR:3 * W_BR]
    o_fox = _fox_paged(z_fox[:, :W_BR], k_new, v_new, lf, *caches, l, page_table)
    heads = lambda a: a.reshape(rows, 1, N_HEADS, HEAD_DIM)
    st = (heads(k_new), heads(v_new), lf.reshape(rows, 1, N_HEADS), s_new, z_rw)
    return o_rw.astype(BF16), o_fox, st


def kernel(x_prompt, x_sample, c_prompt, c_sample, cache_k, cache_v, cache_logf, state_rwkv, state_shift, page_table, w_ada, b_ada, ln_g, ln_b, w_ffn_up, w_ffn_down, w_in, rw_mu, rw_w0, rw_w2, rw_a0, rw_a2, rw_g2, rw_kk, rw_ka, rw_rk, rw_lnw, rw_lnb, fox_bf, w_up_rw, w_up_fox, w_out):
    P = dict(ln_g=ln_g, ln_b=ln_b, rw_mu=rw_mu, rw_w0=rw_w0, rw_w2=rw_w2, rw_a0=rw_a0, rw_a2=rw_a2,
             rw_g2=rw_g2, rw_kk=rw_kk, rw_ka=rw_ka, rw_rk=rw_rk, rw_lnw=rw_lnw, rw_lnb=rw_lnb, fox_bf=fox_bf)
    depth = w_ada.shape[0]
    bp, t, d = x_prompt.shape
    bs, ts, _ = x_sample.shape
    assert ts == 1
    alpha = (2 * depth) ** 0.25
    rows = 16
    c_all = jnp.zeros((rows, d), F32).at[:bp].set(c_prompt).at[bp:bp + bs].set(c_sample)
    mods = _ada_mods(c_all, w_ada, b_ada).reshape(depth, rows, N_MOD, d)
    xp = x_prompt.reshape(bp * t, d)
    xs = x_sample.reshape(bs, d)
    outs_p, outs_s = [], []
    for l in range(depth):
        lw = _prep_layer_weights(w_ffn_up[l], w_ffn_down[l], w_in[l], w_up_rw[l], w_up_fox[l], w_out[l], 512)
        xp, st = _layer(xp, mods[l, :bp], lw, P, l, t, alpha,
                        functools.partial(_mixer_prompt, P=P, l=l, batch=bp))
        outs_p.append(st)
        xs, st = _layer(xs, mods[l, bp:bp + bs], lw, P, l, 1, alpha,
                        functools.partial(_mixer_sample, P=P, l=l, shift0=state_shift[l], state0=state_rwkv[l],
                                          caches=(cache_k, cache_v, cache_logf), page_table=page_table))
        outs_s.append(st)
    stack = lambda outs, i: jnp.stack([o[i] for o in outs])
    return (xp.reshape(bp, t, d), xs.reshape(bs, 1, d),
            *(stack(outs_p, i) for i in range(5)), *(stack(outs_s, i) for i in range(5)))
```

```python
import functools

import jax
import jax.numpy as jnp
from jax import lax
from jax.experimental import pallas as pl
from jax.experimental.pallas import tpu as pltpu

F32 = jnp.float32
BF16 = jnp.bfloat16

HEAD_DIM = 64
N_HEADS = 16
W_BR = N_HEADS * HEAD_DIM
LORA_W, LORA_A, LORA_G = 64, 64, 128
RW_COLS = 3 * W_BR + LORA_W + LORA_A + LORA_G
FOX_COLS = 3 * W_BR + N_HEADS
FOX_PAD = 3 * W_BR + 128
N_MOD = 9
LN_EPS = 1e-5
GN_EPS = 64e-5
LANE = 128
CHUNK = 64
SUB = 16
VMEM_LIMIT = 52 * 1024 * 1024


def _tile(n, pref, step=8):
    if n <= pref:
        return n
    for t in range(pref - pref % step, step - 1, -step):
        if n % t == 0:
            return t
    raise ValueError(f"no tile for {n}")


def _cparams(sem):
    return pltpu.CompilerParams(dimension_semantics=sem, vmem_limit_bytes=VMEM_LIMIT)


def _sigmoid(x):
    return 1.0 / (1.0 + jnp.exp(-x))


def _silu(x):
    return x * _sigmoid(x)


def _log_sigmoid(x):
    return -(jnp.maximum(-x, 0.0) + jnp.log1p(jnp.exp(-jnp.abs(x))))


def _layer_norm(y, g, b):
    mu = jnp.mean(y, axis=-1, keepdims=True)
    d = y - mu
    var = jnp.mean(d * d, axis=-1, keepdims=True)
    return d * lax.rsqrt(var + LN_EPS) * g + b


def _dot(a, b):
    return jnp.dot(a, b, preferred_element_type=F32)


def _ada_kernel(c_ref, w_ref, b_ref, o_ref):
    h = _silu(c_ref[...]).astype(BF16)
    o_ref[0] = _dot(h, w_ref[0].astype(BF16)) + b_ref[0]


def _ada_mods(c_all, w_ada, b_ada):
    depth, d, nm = w_ada.shape
    r = c_all.shape[0]
    tn = _tile(nm, 1024, LANE)
    return pl.pallas_call(
        _ada_kernel,
        grid=(depth, nm // tn),
        in_specs=[pl.BlockSpec((r, d), lambda l, j: (0, 0)),
                  pl.BlockSpec((1, d, tn), lambda l, j: (l, 0, j)),
                  pl.BlockSpec((1, 1, tn), lambda l, j: (l, 0, j))],
        out_specs=pl.BlockSpec((1, r, tn), lambda l, j: (l, 0, j)),
        out_shape=jax.ShapeDtypeStruct((depth, r, nm), F32),
        compiler_params=_cparams(("arbitrary", "arbitrary")),
        name="ada_mods",
    )(c_all, w_ada, b_ada.reshape(depth, 1, nm))


def _mod_spec(rows_per_batch, tm, d):
    if rows_per_batch == 1:
        return pl.BlockSpec((1, tm, d), lambda i, *_: (0, 0, 0))
    tiles = rows_per_batch // tm
    return pl.BlockSpec((1, 1, d), lambda i, *_: (i // tiles, 0, 0))


def _mod_arrays(mods, idx, rows_per_batch):
    m = mods[:, idx]
    return m[None] if rows_per_batch == 1 else m[:, None]


def _ffn_kernel(x_ref, sh_ref, sc_ref, gt_ref, wa_ref, wb_ref, wd_ref, g_ref, b_ref, o_ref,
                h_scr, acc_scr, *, alpha):
    f = pl.program_id(1)

    @pl.when(f == 0)
    def _():
        h_scr[...] = (x_ref[...] * (1.0 + sc_ref[0]) + sh_ref[0]).astype(BF16)
        acc_scr[...] = jnp.zeros_like(acc_scr)

    h = h_scr[...]
    a = _dot(h, wa_ref[...])
    b = _dot(h, wb_ref[...])
    acc_scr[...] += _dot((_silu(a) * b).astype(BF16), wd_ref[...])

    @pl.when(f == pl.num_programs(1) - 1)
    def _():
        y = alpha * x_ref[...] + 0.5 * gt_ref[0] * acc_scr[...]
        o_ref[...] = _layer_norm(y, g_ref[...], b_ref[...])


def _ffn(x, sh, sc, gt, w_up, w_down, ln_g, ln_b, rows_per_batch, alpha, tm, tf):
    m, d = x.shape
    fp = w_down.shape[0]
    nf = fp // tf
    ms = _mod_spec(rows_per_batch, tm, d)
    return pl.pallas_call(
        functools.partial(_ffn_kernel, alpha=alpha),
        grid=(m // tm, nf),
        in_specs=[pl.BlockSpec((tm, d), lambda i, f: (i, 0)), ms, ms, ms,
                  pl.BlockSpec((d, tf), lambda i, f: (0, f)),
                  pl.BlockSpec((d, tf), lambda i, f: (0, nf + f)),
                  pl.BlockSpec((tf, d), lambda i, f: (f, 0)),
                  pl.BlockSpec((1, d), lambda i, f: (0, 0)),
                  pl.BlockSpec((1, d), lambda i, f: (0, 0))],
        out_specs=pl.BlockSpec((tm, d), lambda i, f: (i, 0)),
        out_shape=jax.ShapeDtypeStruct((m, d), F32),
        scratch_shapes=[pltpu.VMEM((tm, d), BF16), pltpu.VMEM((tm, d), F32)],
        compiler_params=_cparams(("arbitrary", "arbitrary")),
        name="ffn",
    )(x, sh, sc, gt, w_up, w_up, w_down, ln_g.reshape(1, d), ln_b.reshape(1, d))


def _proj_kernel(x_ref, sh_ref, sc_ref, w_ref, o_ref, h_scr, *, act):
    @pl.when(pl.program_id(1) == 0)
    def _():
        h_scr[...] = (x_ref[...] * (1.0 + sc_ref[0]) + sh_ref[0]).astype(BF16)

    z = _dot(h_scr[...], w_ref[...])
    o_ref[...] = _sigmoid(z) if act == "sigmoid" else z


def _proj(x, sh, sc, w, rows_per_batch, tm, tn, act=None):
    m, d = x.shape
    n = w.shape[1]
    ms = _mod_spec(rows_per_batch, tm, d)
    return pl.pallas_call(
        functools.partial(_proj_kernel, act=act),
        grid=(m // tm, n // tn),
        in_specs=[pl.BlockSpec((tm, d), lambda i, j: (i, 0)), ms, ms,
                  pl.BlockSpec((d, tn), lambda i, j: (0, j))],
        out_specs=pl.BlockSpec((tm, tn), lambda i, j: (i, j)),
        out_shape=jax.ShapeDtypeStruct((m, n), F32),
        scratch_shapes=[pltpu.VMEM((tm, d), BF16)],
        compiler_params=_cparams(("arbitrary", "arbitrary")),
        name="proj",
    )(x, sh, sc, w)


def _merge_kernel(x_ref, gt_ref, orw_ref, ofox_ref, grw_ref, gfox_ref, wur_ref, wuf_ref, wo_ref,
                  g_ref, b_ref, o_ref, *, alpha):
    m = (grw_ref[...] * _dot(orw_ref[...], wur_ref[...])
         + gfox_ref[...] * _dot(ofox_ref[...], wuf_ref[...]))
    y = alpha * x_ref[...] + gt_ref[0] * _dot(m.astype(BF16), wo_ref[...])
    o_ref[...] = _layer_norm(y, g_ref[...], b_ref[...])


def _merge(x, gt, o_rw, o_fox, gates, w_up_rw, w_up_fox, w_out, ln_g, ln_b, rows_per_batch, alpha, tm):
    m, d = x.shape
    wb = o_rw.shape[1]
    ms = _mod_spec(rows_per_batch, tm, d)
    const = lambda shape: pl.BlockSpec(shape, lambda i: (0, 0), pipeline_mode=pl.Buffered(1))
    return pl.pallas_call(
        functools.partial(_merge_kernel, alpha=alpha),
        grid=(m // tm,),
        in_specs=[pl.BlockSpec((tm, d), lambda i: (i, 0)), ms,
                  pl.BlockSpec((tm, wb), lambda i: (i, 0)),
                  pl.BlockSpec((tm, wb), lambda i: (i, 0)),
                  pl.BlockSpec((tm, d), lambda i: (i, 0)),
                  pl.BlockSpec((tm, d), lambda i: (i, 1)),
                  const((wb, d)), const((wb, d)), const((d, d)), const((1, d)), const((1, d))],
        out_specs=pl.BlockSpec((tm, d), lambda i: (i, 0)),
        out_shape=jax.ShapeDtypeStruct((m, d), F32),
        compiler_params=_cparams(("arbitrary",)),
        name="merge",
    )(x, gt, o_rw, o_fox, gates, gates, w_up_rw, w_up_fox, w_out, ln_g.reshape(1, d), ln_b.reshape(1, d))


def _rw_prep_math(zs, w0, w2, a0, a2, g2):
    r = zs[:, 0:W_BR]
    k = zs[:, W_BR:2 * W_BR]
    v = zs[:, 2 * W_BR:3 * W_BR]
    c0 = 3 * W_BR
    wd = zs[:, c0:c0 + LORA_W]
    ad = zs[:, c0 + LORA_W:c0 + LORA_W + LORA_A]
    gd = zs[:, c0 + LORA_W + LORA_A:c0 + LORA_W + LORA_A + LORA_G]
    w_log = _log_sigmoid(w0 + _dot(jnp.tanh(wd).astype(BF16), w2)) - 0.5
    log_decay = -jnp.exp(w_log)
    a = _sigmoid(a0 + _dot(ad.astype(BF16), a2))
    g = _dot(_sigmoid(gd).astype(BF16), g2)
    return r, k, v, log_decay, a, g


def _rw_prep_prompt_kernel(z_ref, mu_ref, w0_ref, w2_ref, a0_ref, a2_ref, g2_ref,
                           r_ref, k_ref, v_ref, lw_ref, a_ref, g_ref, carry_scr, *, tiles_per_batch):
    i = pl.program_id(0)
    z = z_ref[...]
    tm = z.shape[0]

    @pl.when(i % tiles_per_batch == 0)
    def _():
        carry_scr[...] = jnp.zeros_like(carry_scr)

    row = lax.broadcasted_iota(jnp.int32, z.shape, 0)
    z_prev = jnp.where(row == 0, carry_scr[...], pltpu.roll(z, 1, 0))
    carry_scr[...] = z[tm - 1:tm, :]
    zs = z + (z_prev - z) * mu_ref[...]
    outs = _rw_prep_math(zs, w0_ref[...], w2_ref[...], a0_ref[...], a2_ref[...], g2_ref[...])
    for ref, val in zip((r_ref, k_ref, v_ref, lw_ref, a_ref, g_ref), outs):
        ref[...] = val


def _rw_prep_prompt(z_rw, batch, mu, w0, w2, a0, a2, g2, tm):
    m, cols = z_rw.shape
    t = m // batch
    tiles = t // tm
    vec = lambda n: pl.BlockSpec((1, n), lambda i: (0, 0))
    mat = lambda a: pl.BlockSpec(a.shape, lambda i: (0, 0))
    hm = pl.BlockSpec((tm, W_BR), lambda i: (i, 0))
    hm_shape = jax.ShapeDtypeStruct((m, W_BR), F32)
    return pl.pallas_call(
        functools.partial(_rw_prep_prompt_kernel, tiles_per_batch=tiles),
        grid=(m // tm,),
        in_specs=[pl.BlockSpec((tm, cols), lambda i: (i, 0)), vec(cols), vec(W_BR), mat(w2),
                  vec(W_BR), mat(a2), mat(g2)],
        out_specs=[hm] * 6,
        out_shape=[hm_shape] * 6,
        scratch_shapes=[pltpu.VMEM((1, cols), F32)],
        compiler_params=_cparams(("arbitrary",)),
        name="rw_prep_prompt",
    )(z_rw, mu.reshape(1, cols), w0.reshape(1, W_BR), w2, a0.reshape(1, W_BR), a2, g2)


def _rw_prep_sample_kernel(z_ref, prev_ref, mu_ref, w0_ref, w2_ref, a0_ref, a2_ref, g2_ref,
                           r_ref, k_ref, v_ref, lw_ref, a_ref, g_ref):
    z = z_ref[...]
    zs = z + (prev_ref[...] - z) * mu_ref[...]
    outs = _rw_prep_math(zs, w0_ref[...], w2_ref[...], a0_ref[...], a2_ref[...], g2_ref[...])
    for ref, val in zip((r_ref, k_ref, v_ref, lw_ref, a_ref, g_ref), outs):
        ref[...] = val


def _rw_prep_sample(z_rw, shift0, mu, w0, w2, a0, a2, g2):
    rows, cols = z_rw.shape
    full = lambda a: pl.BlockSpec(a.shape, lambda: (0,) * a.ndim)
    args = (z_rw, shift0, mu.reshape(1, cols), w0.reshape(1, W_BR), w2, a0.reshape(1, W_BR), a2, g2)
    out = jax.ShapeDtypeStruct((rows, W_BR), F32)
    return pl.pallas_call(
        _rw_prep_sample_kernel,
        in_specs=[full(a) for a in args],
        out_specs=[pl.BlockSpec((rows, W_BR), lambda: (0, 0))] * 6,
        out_shape=[out] * 6,
        compiler_params=pltpu.CompilerParams(vmem_limit_bytes=VMEM_LIMIT),
        name="rw_prep_sample",
    )(*args)


def _split_dot(x, w, passes):
    acc, rem = None, x
    for i in range(passes):
        piece = rem.astype(BF16)
        term = _dot(piece, w)
        acc = term if acc is None else acc + term
        if i + 1 < passes:
            rem = rem - piece.astype(F32)
    return acc


def _split_dot_rhs(w, x, passes):
    acc, rem = None, x
    for i in range(passes):
        piece = rem.astype(BF16)
        term = _dot(w, piece)
        acc = term if acc is None else acc + term
        if i + 1 < passes:
            rem = rem - piece.astype(F32)
    return acc


def _rwkv_chunk_kernel(r_ref, k_ref, v_ref, lw_ref, a_ref, g_ref, kkw_ref, ka_ref, rk_ref, lnw_ref,
                       lnb_ref, o_ref, s_ref, s_scr, *, pairs):
    tb = r_ref.shape[0]
    c, n, pw = CHUNK, HEAD_DIM, 2 * HEAD_DIM
    assert c == n

    @pl.when(pl.program_id(2) == 0)
    def _():
        s_scr[...] = jnp.zeros_like(s_scr)

    row = lax.broadcasted_iota(jnp.int32, (c, pw), 0)
    lane = lax.broadcasted_iota(jnp.int32, (c, pw), 1)
    first = lane < n
    col = jnp.where(first, lane, lane - n)
    lower = col <= row
    strict = col < row
    same_sub = (row // SUB) == (col // SUB)
    eye = jnp.where(row == col, 1.0, 0.0).astype(F32)
    tr = lax.broadcasted_iota(jnp.int32, (c, c), 0)
    tc = lax.broadcasted_iota(jnp.int32, (c, c), 1)
    tri = jnp.where(tc <= tr, 1.0, 0.0).astype(BF16)
    br = lax.broadcasted_iota(jnp.int32, (pw, pw), 0) // n
    bc = lax.broadcasted_iota(jnp.int32, (pw, pw), 1) // n
    same_head = br == bc
    ones_bd = jnp.where(same_head, 1.0, 0.0).astype(BF16)

    def expand(x):
        return jnp.concatenate([jnp.where(first, x, 0.0), jnp.where(first, 0.0, x)], axis=0).astype(BF16)

    def mm(x, y):
        return _dot(x.astype(BF16), expand(y))

    nt = lambda x, y: lax.dot_general(x, y, (((1,), (1,)), ((), ())), preferred_element_type=F32)

    def chunk(ci, carry):
        sl = pl.ds(pl.multiple_of(ci * c, c), c)
        ps_ = range(pairs)
        each = lambda f, *xs: [f(*args) for args in zip(*xs)]
        cols = [slice(p * pw, (p + 1) * pw) for p in ps_]
        r, k, v, lw, a, g = ([ref[sl, cs] for cs in cols] for ref in (r_ref, k_ref, v_ref, lw_ref, a_ref, g_ref))
        kkw, ka, rk, lnw, lnb = ([ref[:, cs] for cs in cols] for ref in (kkw_ref, ka_ref, rk_ref, lnw_ref, lnb_ref))
        s0 = [s_scr[p] for p in ps_]
        kk = each(lambda x, w: x * w, k, kkw)
        nrm = each(lambda x: _split_dot(x * x, ones_bd, 2), kk)
        kk = each(lambda x, n2: x / jnp.maximum(jnp.sqrt(n2), 1e-12), kk, nrm)
        k_mod = each(lambda x, ai, w: x * (1.0 + (ai - 1.0) * w), k, a, ka)
        be = each(lambda x, ai: x * ai, kk, a)
        cum = each(lambda x: _split_dot_rhs(tri, x, 3), lw)
        tot = each(lambda x: x[c - 1:c, :], cum)
        w_inv = each(lambda x: jnp.exp(-x), cum)
        w_end = each(lambda x, t: jnp.exp(t - x), cum, tot)
        ab = each(lambda x, cu, l: -x * jnp.exp(cu - l), kk, cum, lw)
        rb = each(lambda x, cu: x * jnp.exp(cu), r, cum)
        lhs = each(lambda x, y: jnp.concatenate([x, y], axis=0).astype(BF16), ab, rb)
        big = each(lambda x, b_, km, wi: nt(x, jnp.concatenate([expand(b_ * wi), expand(km * wi)], axis=0)),
                   lhs, be, k_mod, w_inv)
        from_s = each(lambda x, s: nt(x, s.astype(BF16)), lhs, s0)
        l_mat = each(lambda x: jnp.where(strict, x[:c, :pw], 0.0), big)
        m_rb = each(lambda x: jnp.where(lower, x[c:, :pw], 0.0), big)
        m_k = each(lambda x: jnp.concatenate([jnp.where(strict, x[:c, pw:], 0.0),
                                              jnp.where(lower, x[c:, pw:], 0.0)], axis=0).astype(BF16), big)
        from_v = each(lambda x, vi: _dot(x, expand(vi)), m_k, v)
        rhs = each(lambda x, y: x[:c] + y[:c], from_s, from_v)
        l_d = each(lambda x: jnp.where(same_sub, x, 0.0), l_mat)
        l_o = each(lambda x, y: x - y, l_mat, l_d)
        p2 = each(mm, l_d, l_d)
        p4 = each(mm, p2, p2)
        p8 = each(mm, p4, p4)
        t_d = each(lambda x: eye + x, l_d)
        t_d = each(lambda t, p_: t + mm(p_, t), t_d, p2)
        t_d = each(lambda t, p_: t + mm(p_, t), t_d, p4)
        t_d = each(lambda t, p_: t + mm(p_, t), t_d, p8)
        q = each(mm, t_d, l_o)
        u = each(mm, t_d, rhs)
        q2 = each(mm, q, q)
        u = each(lambda x, y: x + mm(y, x), u, q2)
        u = each(lambda x, y: x + mm(y, x), u, q)
        o = each(lambda x, y, m_, ui: x[c:] + y[c:] + mm(m_, ui), from_s, from_v, m_rb, u)
        upd = each(lambda ui, vi, b_, km, we: lax.dot_general(
            jnp.concatenate([ui, vi], axis=0).astype(BF16),
            jnp.concatenate([b_ * we, km * we], axis=0).astype(BF16),
            (((0,), (0,)), ((), ())), preferred_element_type=F32), u, v, be, k_mod, w_end)
        s_new = each(lambda s, t, x: s * jnp.exp(t) + jnp.where(same_head, x, 0.0), s0, tot, upd)
        mu = each(lambda x: _split_dot(x, ones_bd, 2) * (1.0 / n), o)
        d = each(lambda x, y: x - y, o, mu)
        var = each(lambda x: _split_dot(x * x, ones_bd, 2) * (1.0 / n), d)
        bonus = each(lambda ri, km, w, vi: _split_dot(ri * km * w, ones_bd, 2) * vi, r, k_mod, rk, v)
        out = each(lambda di, va, w, b_, bo, gi: (di * lax.rsqrt(va + GN_EPS) * w + b_ + bo) * gi,
                   d, var, lnw, lnb, bonus, g)
        for p in ps_:
            o_ref[sl, cols[p]] = out[p].astype(o_ref.dtype)
            s_scr[p] = s_new[p]
        return carry

    lax.fori_loop(0, tb // c, chunk, 0)

    @pl.when(pl.program_id(2) == pl.num_programs(2) - 1)
    def _():
        for p in range(pairs):
            s = s_scr[p]
            s_ref[0, 2 * p] = s[:n, :n]
            s_ref[0, 2 * p + 1] = s[n:, n:]


def _rwkv_prompt_scan(r, k, v, lw, a, g, kkw, ka, rk, lnw, lnb, batch, tb, pairs):
    m, w = r.shape
    t = m // batch
    nt_ = t // tb
    gw = pairs * 2 * HEAD_DIM
    seq = pl.BlockSpec((tb, gw), lambda bi, gi, ti: (bi * nt_ + ti, gi))
    par = pl.BlockSpec((1, gw), lambda bi, gi, ti: (0, gi))
    vec = lambda x: x.reshape(1, w)
    return pl.pallas_call(
        functools.partial(_rwkv_chunk_kernel, pairs=pairs),
        grid=(batch, w // gw, nt_),
        in_specs=[seq] * 6 + [par] * 5,
        out_specs=[seq, pl.BlockSpec((1, 2 * pairs, HEAD_DIM, HEAD_DIM), lambda bi, gi, ti: (bi, gi, 0, 0))],
        out_shape=[jax.ShapeDtypeStruct((m, w), BF16),
                   jax.ShapeDtypeStruct((batch, N_HEADS, HEAD_DIM, HEAD_DIM), F32)],
        scratch_shapes=[pltpu.VMEM((pairs, 2 * HEAD_DIM, 2 * HEAD_DIM), F32)],
        compiler_params=_cparams(("arbitrary", "arbitrary", "arbitrary")),
        name="rwkv_chunk",
    )(r, k, v, lw, a, g, vec(kkw), vec(ka), vec(rk), vec(lnw), vec(lnb))


def _rwkv_step_kernel(p_ref, r_ref, k_ref, lw_ref, a_ref, v_ref, g_ref, kkw_ref, ka_ref, rk_ref,
                      lnw_ref, lnb_ref, o_ref, pn_ref):
    p = p_ref[0]
    r, k, lw, a = r_ref[0], k_ref[0], lw_ref[0], a_ref[0]
    v, g = v_ref[0], g_ref[0]
    kk = k * kkw_ref[...]
    kk = kk / jnp.maximum(jnp.sqrt(jnp.sum(kk * kk, axis=1, keepdims=True)), 1e-12)
    k_mod = k * (1.0 + (a - 1.0) * ka_ref[...])
    sa = jnp.sum(p * (-kk), axis=1, keepdims=True)
    p_new = p * jnp.exp(lw) + (kk * a) * sa + k_mod * v
    pn_ref[0] = p_new
    o = jnp.sum(p_new * r, axis=1, keepdims=True)
    mu = jnp.mean(o, axis=-1, keepdims=True)
    d = o - mu
    var = jnp.mean(d * d, axis=-1, keepdims=True)
    bonus = jnp.sum(r * k_mod * rk_ref[...], axis=1, keepdims=True) * v
    o_ref[0] = (d * lax.rsqrt(var + GN_EPS) * lnw_ref[...] + lnb_ref[...] + bonus) * g


def _rwkv_sample_step(state, r, k, v, lw, a, g, kkw, ka, rk, lnw, lnb):
    rows = state.shape[0]
    h, n = N_HEADS, HEAD_DIM
    col = lambda x: x.reshape(rows, h, n, 1)
    rowv = lambda x: x.reshape(rows, h, 1, n)
    pcol = lambda x: x.reshape(h, n, 1)
    prow = lambda x: x.reshape(h, 1, n)
    cs = pl.BlockSpec((1, h, n, 1), lambda i: (i, 0, 0, 0))
    rs = pl.BlockSpec((1, h, 1, n), lambda i: (i, 0, 0, 0))
    ss = pl.BlockSpec((1, h, n, n), lambda i: (i, 0, 0, 0))
    pc = pl.BlockSpec((h, n, 1), lambda i: (0, 0, 0))
    pr = pl.BlockSpec((h, 1, n), lambda i: (0, 0, 0))
    o, p_new = pl.pallas_call(
        _rwkv_step_kernel,
        grid=(rows,),
        in_specs=[ss, cs, cs, cs, cs, rs, rs, pc, pc, pc, pr, pr],
        out_specs=[rs, ss],
        out_shape=[jax.ShapeDtypeStruct((rows, h, 1, n), F32), jax.ShapeDtypeStruct((rows, h, n, n), F32)],
        compiler_params=_cparams(("arbitrary",)),
        name="rwkv_step",
    )(jnp.swapaxes(state, 2, 3), col(r), col(k), col(lw), col(a), rowv(v), rowv(g),
      pcol(kkw), pcol(ka), pcol(rk), prow(lnw), prow(lnb))
    return o.reshape(rows, h * n), jnp.swapaxes(p_new, 2, 3)


N_BIAS = 3
AUG = 2 * HEAD_DIM


def _fox_prep_kernel(q_ref, k_ref, v_ref, zf_ref, bf_ref, qa_ref, ka_ref, vh_ref, lf_ref, carry_scr,
                     *, tiles_per_batch):
    @pl.when(pl.program_id(0) % tiles_per_batch == 0)
    def _():
        carry_scr[...] = jnp.zeros_like(carry_scr)

    lf = _log_sigmoid(zf_ref[...][:, :N_HEADS] + bf_ref[...])
    tm = lf.shape[0]
    row = lax.broadcasted_iota(jnp.int32, (tm, tm), 0)
    col = lax.broadcasted_iota(jnp.int32, (tm, tm), 1)
    cum = _split_dot_rhs(jnp.where(col <= row, 1.0, 0.0).astype(BF16), lf, 3) + carry_scr[...]
    carry_scr[...] = cum[tm - 1:tm, :]
    lf_ref[...] = lf
    pieces, rem = [], cum
    for _ in range(N_BIAS):
        piece = rem.astype(BF16).astype(F32)
        pieces.append(piece)
        rem = rem - piece
    lane = lax.broadcasted_iota(jnp.int32, (tm, HEAD_DIM), 1)
    q = q_ref[...] * HEAD_DIM ** -0.5
    k = k_ref[...]
    v = v_ref[...]
    for h in range(N_HEADS):
        hs = slice(h * HEAD_DIM, (h + 1) * HEAD_DIM)
        ext_q = jnp.where(lane < 2 * N_BIAS, 1.0, 0.0)
        ext_k = ext_q
        for i, piece in enumerate(pieces):
            f = piece[:, h:h + 1]
            ext_q = jnp.where(lane == i, f, ext_q)
            ext_k = jnp.where(lane == N_BIAS + i, -f, ext_k)
        qa_ref[h] = jnp.concatenate([q[:, hs], ext_q], axis=1).astype(BF16)
        ka_ref[h] = jnp.concatenate([k[:, hs], ext_k], axis=1).astype(BF16)
        vh_ref[h] = jnp.concatenate([v[:, hs], jnp.where(lane == 0, 1.0, 0.0)], axis=1).astype(BF16)


def _fox_prep(z_fox, batch, bf, tm):
    m = z_fox.shape[0]
    tiles = (m // batch) // tm
    blk = lambda j: pl.BlockSpec((tm, W_BR), lambda i: (i, j))
    hm = lambda w: pl.BlockSpec((N_HEADS, tm, w), lambda i: (0, i, 0))
    return pl.pallas_call(
        functools.partial(_fox_prep_kernel, tiles_per_batch=tiles),
        grid=(m // tm,),
        in_specs=[blk(0), blk(1), blk(2),
                  pl.BlockSpec((tm, LANE), lambda i: (i, 3 * W_BR // LANE)),
                  pl.BlockSpec((1, N_HEADS), lambda i: (0, 0))],
        out_specs=[hm(AUG), hm(AUG), hm(AUG), pl.BlockSpec((tm, N_HEADS), lambda i: (i, 0))],
        out_shape=[jax.ShapeDtypeStruct((N_HEADS, m, AUG), BF16), jax.ShapeDtypeStruct((N_HEADS, m, AUG), BF16),
                   jax.ShapeDtypeStruct((N_HEADS, m, AUG), BF16), jax.ShapeDtypeStruct((m, N_HEADS), F32)],
        scratch_shapes=[pltpu.VMEM((1, N_HEADS), F32)],
        compiler_params=_cparams(("arbitrary",)),
        name="fox_prep",
    )(z_fox, z_fox, z_fox, z_fox, bf.reshape(1, N_HEADS))


def _fox_kernel(q_ref, k_ref, v_ref, o_ref, m_scr, acc_scr):
    i, j = pl.program_id(1), pl.program_id(2)
    tq, tk = q_ref.shape[1], k_ref.shape[1]
    nl = tk // LANE

    @pl.when(j == 0)
    def _():
        m_scr[...] = jnp.full_like(m_scr, -jnp.inf)
        acc_scr[...] = jnp.zeros_like(acc_scr)

    def accumulate(on_diagonal):
        if on_diagonal:
            causal = (lax.broadcasted_iota(jnp.int32, (tq, tk), 1)
                      <= lax.broadcasted_iota(jnp.int32, (tq, tk), 0))
        for h in range(N_HEADS):
            s = lax.dot_general(q_ref[h], k_ref[h], (((1,), (1,)), ((), ())), preferred_element_type=F32)
            if on_diagonal:
                s = jnp.where(causal, s, -jnp.inf)
            blk = s[:, :LANE]
            for c in range(1, nl):
                blk = jnp.maximum(blk, s[:, c * LANE:(c + 1) * LANE])
            m_prev = m_scr[h]
            m_new = jnp.maximum(m_prev, jnp.max(blk, axis=-1, keepdims=True))
            p = jnp.exp(s - jnp.concatenate([m_new] * nl, axis=1))
            acc_scr[h] = jnp.exp(m_prev - m_new) * acc_scr[h] + _dot(p.astype(BF16), v_ref[h])
            m_scr[h] = m_new

    @pl.when(j < i)
    def _():
        accumulate(False)

    @pl.when(j == i)
    def _():
        accumulate(True)
        for h in range(N_HEADS):
            acc = acc_scr[h]
            o_ref[:, h * HEAD_DIM:(h + 1) * HEAD_DIM] = (
                acc[:, :HEAD_DIM] / acc[:, HEAD_DIM:HEAD_DIM + 1]).astype(o_ref.dtype)


def _fox_prompt(q_aug, k_aug, v_hm, batch, tq):
    m = q_aug.shape[1]
    t = m // batch
    nq = t // tq
    kv = lambda w: pl.BlockSpec((N_HEADS, tq, w), lambda b, i, j: (0, b * nq + jnp.minimum(j, i), 0))
    return pl.pallas_call(
        _fox_kernel,
        grid=(batch, nq, nq),
        in_specs=[pl.BlockSpec((N_HEADS, tq, AUG), lambda b, i, j: (0, b * nq + i, 0)), kv(AUG), kv(AUG)],
        out_specs=pl.BlockSpec((tq, W_BR), lambda b, i, j: (b * nq + i, 0)),
        out_shape=jax.ShapeDtypeStruct((m, W_BR), BF16),
        scratch_shapes=[pltpu.VMEM((N_HEADS, tq, LANE), F32), pltpu.VMEM((N_HEADS, tq, AUG), F32)],
        compiler_params=_cparams(("arbitrary", "arbitrary", "arbitrary")),
        name="fox_prompt",
    )(q_aug, k_aug, v_hm)


PAGES_PER_STEP = 8


def _fox_paged_kernel(pt_ref, q_ref, kn_ref, vn_ref, lfn_ref, *refs):
    del pt_ref
    npg = PAGES_PER_STEP
    k_refs, v_refs, lf_refs = refs[:npg], refs[npg:2 * npg], refs[2 * npg:3 * npg]
    o_ref, m_scr, l_scr, acc_scr, suf_scr = refs[3 * npg:]
    step = pl.program_id(1)
    ps = lf_refs[0].shape[3]
    nh = N_HEADS
    scale = HEAD_DIM ** -0.5
    q = q_ref[0]
    qb = q.astype(BF16)
    head = lax.broadcasted_iota(jnp.int32, (nh, 1), 0)
    later = (lax.broadcasted_iota(jnp.int32, (ps, ps), 0) > lax.broadcasted_iota(jnp.int32, (ps, ps), 1))
    later = jnp.where(later, 1.0, 0.0).astype(BF16)

    @pl.when(step == 0)
    def _():
        m_scr[...] = jnp.sum(q * kn_ref[0], axis=-1, keepdims=True) * scale
        l_scr[...] = jnp.ones_like(l_scr)
        acc_scr[...] = vn_ref[0]
        suf_scr[...] = lfn_ref[0]

    suf = suf_scr[...]
    scores = []
    for i in range(npg):
        lft = lf_refs[i][0, 0]
        after = _split_dot(lft, later, 3) + suf
        suf = suf + jnp.sum(lft, axis=-1, keepdims=True)
        s = jnp.zeros((nh, ps), F32)
        for h in range(nh):
            sh = _dot(qb, k_refs[i][0, 0, h].astype(BF16))
            s = jnp.where(head == h, sh, s)
        scores.append(s * scale + after)
    suf_scr[...] = suf
    m_prev = m_scr[...]
    m_new = m_prev
    for s in scores:
        m_new = jnp.maximum(m_new, jnp.max(s, axis=-1, keepdims=True))
    corr = jnp.exp(m_prev - m_new)
    l_new = corr * l_scr[...]
    pv = jnp.zeros((nh, HEAD_DIM), F32)
    for i in range(npg):
        p = jnp.exp(scores[i] - m_new)
        l_new = l_new + jnp.sum(p, axis=-1, keepdims=True)
        pb = p.astype(BF16)
        for h in range(nh):
            oh = lax.dot_general(pb, v_refs[i][0, 0, h].astype(BF16), (((1,), (1,)), ((), ())),
                                 preferred_element_type=F32)
            pv = pv + jnp.where(head == h, oh, 0.0)
    l_scr[...] = l_new
    acc_scr[...] = corr * acc_scr[...] + pv
    m_scr[...] = m_new

    @pl.when(step == pl.num_programs(1) - 1)
    def _():
        o_ref[0] = (acc_scr[...] / l_scr[...]).astype(o_ref.dtype)


def _fox_paged(q, k_new, v_new, lf_new, cache_kt, cache_vt, cache_lft, layer, page_table):
    rows, n_pages = page_table.shape
    ps = cache_lft.shape[3]
    npg = PAGES_PER_STEP
    assert n_pages % npg == 0
    nh, hd = N_HEADS, HEAD_DIM
    heads = lambda x: x.reshape(rows, nh, hd)

    def page(i, nd):
        return lambda b, p, pt: (layer, pt[b, n_pages - 1 - (p * npg + i)]) + (0,) * nd

    vec = pl.BlockSpec((1, nh, hd), lambda b, p, pt: (b, 0, 0))
    grid_spec = pltpu.PrefetchScalarGridSpec(
        num_scalar_prefetch=1,
        grid=(rows, n_pages // npg),
        in_specs=([vec, vec, vec, pl.BlockSpec((1, nh, 1), lambda b, p, pt: (b, 0, 0))]
                  + [pl.BlockSpec((1, 1, nh, hd, ps), page(i, 3)) for i in range(npg)]
                  + [pl.BlockSpec((1, 1, nh, hd, ps), page(i, 3)) for i in range(npg)]
                  + [pl.BlockSpec((1, 1, nh, ps), page(i, 2)) for i in range(npg)]),
        out_specs=vec,
        scratch_shapes=[pltpu.VMEM((nh, 1), F32), pltpu.VMEM((nh, 1), F32),
                        pltpu.VMEM((nh, hd), F32), pltpu.VMEM((nh, 1), F32)],
    )
    out = pl.pallas_call(
        _fox_paged_kernel,
        grid_spec=grid_spec,
        out_shape=jax.ShapeDtypeStruct((rows, nh, hd), BF16),
        compiler_params=_cparams(("arbitrary", "arbitrary")),
        name="fox_paged",
    )(page_table, heads(q), heads(k_new), heads(v_new), lf_new.reshape(rows, nh, 1),
      *([cache_kt] * npg), *([cache_vt] * npg), *([cache_lft] * npg))
    return out.reshape(rows, nh * hd)


def _forget_sample_kernel(zf_ref, bf_ref, lf_ref):
    lf_ref[...] = _log_sigmoid(zf_ref[...][:, :N_HEADS] + bf_ref[...])


def _forget_sample(z_fox, bf):
    rows = z_fox.shape[0]
    return pl.pallas_call(
        _forget_sample_kernel,
        grid=(1,),
        in_specs=[pl.BlockSpec((rows, LANE), lambda i: (0, 3 * W_BR // LANE)),
                  pl.BlockSpec((1, N_HEADS), lambda i: (0, 0))],
        out_specs=pl.BlockSpec((rows, N_HEADS), lambda i: (0, 0)),
        out_shape=jax.ShapeDtypeStruct((rows, N_HEADS), F32),
        name="forget_sample",
    )(z_fox, bf.reshape(1, N_HEADS))


def _pad_cols(w, n):
    return jnp.pad(w, ((0, 0), (0, n - w.shape[1])))


def _prep_layer_weights(w_ffn_up, w_ffn_down, w_in, w_up_rw, w_up_fox, w_out, tf):
    d_ff = w_ffn_down.shape[1]
    fp = -(-d_ff // tf) * tf
    ups, downs = [], []
    for j in range(w_ffn_up.shape[0]):
        wa = _pad_cols(w_ffn_up[j][:, :d_ff], fp)
        wb = _pad_cols(w_ffn_up[j][:, d_ff:], fp)
        ups.append(jnp.concatenate([wa, wb], axis=1).astype(BF16))
        downs.append(jnp.pad(w_ffn_down[j], ((0, fp - d_ff), (0, 0))).astype(BF16))
    d = w_in.shape[0]
    w_rw = w_in[:, :RW_COLS].astype(BF16)
    w_fox = _pad_cols(w_in[:, RW_COLS:RW_COLS + FOX_COLS], FOX_PAD).astype(BF16)
    w_gate = w_in[:, RW_COLS + FOX_COLS:].astype(BF16)
    assert w_gate.shape[1] == 2 * d
    return dict(up=ups, down=downs, rw=w_rw, fox=w_fox, gate=w_gate, up_rw=w_up_rw.astype(BF16),
                up_fox=w_up_fox.astype(BF16), out=w_out.astype(BF16))


def _layer(x, mods, lw, P, l, rows_per_batch, alpha, mixer_fn):
    m, d = x.shape
    seq = m if rows_per_batch == 1 else rows_per_batch
    tm_ffn = _tile(seq, 512)
    tm = _tile(seq, 1024)
    tf = min(512, lw["down"][0].shape[0])
    mod = lambda i: _mod_arrays(mods, i, rows_per_batch)
    x = _ffn(x, mod(0), mod(1), mod(2), lw["up"][0], lw["down"][0], P["ln_g"][l, 0], P["ln_b"][l, 0],
             rows_per_batch, alpha, tm_ffn, tf)
    sh, sc = mod(3), mod(4)
    z_rw = _proj(x, sh, sc, lw["rw"], rows_per_batch, tm, RW_COLS // 2)
    z_fox = _proj(x, sh, sc, lw["fox"], rows_per_batch, tm, FOX_PAD // 5)
    gates = _proj(x, sh, sc, lw["gate"], rows_per_batch, tm, _tile(d, 1024, LANE), act="sigmoid")
    o_rw, o_fox, st = mixer_fn(z_rw, z_fox)
    x = _merge(x, mod(5), o_rw, o_fox, gates, lw["up_rw"], lw["up_fox"], lw["out"],
               P["ln_g"][l, 1], P["ln_b"][l, 1], rows_per_batch, alpha, _tile(seq, 256))
    x = _ffn(x, mod(6), mod(7), mod(8), lw["up"][1], lw["down"][1], P["ln_g"][l, 2], P["ln_b"][l, 2],
             rows_per_batch, alpha, tm_ffn, tf)
    return x, st


def _mixer_prompt(z_rw, z_fox, P, l, batch):
    m = z_rw.shape[0]
    t = m // batch
    heads = lambda a: a.reshape(batch, t, N_HEADS, HEAD_DIM)
    rw = _rw_prep_prompt(z_rw, batch, P["rw_mu"][l], P["rw_w0"][l], P["rw_w2"][l].astype(BF16),
                         P["rw_a0"][l], P["rw_a2"][l].astype(BF16), P["rw_g2"][l].astype(BF16), _tile(t, 256))
    o_rw, s_fin = _rwkv_prompt_scan(*rw, P["rw_kk"][l], P["rw_ka"][l], P["rw_rk"][l].reshape(-1),
                                    P["rw_lnw"][l], P["rw_lnb"][l], batch, _tile(t, 512, CHUNK), 8)
    q_aug, k_aug, v_hm, lf = _fox_prep(z_fox, batch, P["fox_bf"][l], _tile(t, 256))
    o_fox = _fox_prompt(q_aug, k_aug, v_hm, batch, _tile(t, 512))
    shift = z_rw.reshape(batch, t, RW_COLS)[:, -1]
    st = (heads(z_fox[:, W_BR:2 * W_BR]), heads(z_fox[:, 2 * W_BR:3 * W_BR]),
          lf.reshape(batch, t, N_HEADS), s_fin, shift)
    return o_rw, o_fox, st


def _mixer_sample(z_rw, z_fox, P, l, shift0, state0, caches, page_table):
    rows = z_rw.shape[0]
    rw = _rw_prep_sample(z_rw, shift0, P["rw_mu"][l], P["rw_w0"][l], P["rw_w2"][l].astype(BF16),
                         P["rw_a0"][l], P["rw_a2"][l].astype(BF16), P["rw_g2"][l].astype(BF16))
    o_rw, s_new = _rwkv_sample_step(state0, *rw, P["rw_kk"][l], P["rw_ka"][l], P["rw_rk"][l].reshape(-1),
                                    P["rw_lnw"][l], P["rw_lnb"][l])
    lf = _forget_sample(z_fox, P["fox_bf"][l])
    k_new = z_fox[:, W_BR:2 * W_BR]
    v_new = z_fox[:, 2 * W_BR:3 * W_BR]
    o_fox = _fox_paged(z_fox[:, :W_BR], k_new, v_new, lf, *caches, l, page_table)
    heads = lambda a: a.reshape(rows, 1, N_HEADS, HEAD_DIM)
    st = (heads(k_new), heads(v_new), lf.reshape(rows, 1, N_HEADS), s_new, z_rw)
    return o_rw.astype(BF16), o_fox, st


def kernel(x_prompt, x_sample, c_prompt, c_sample, cache_k, cache_v, cache_logf, state_rwkv, state_shift, page_table, w_ada, b_ada, ln_g, ln_b, w_ffn_up, w_ffn_down, w_in, rw_mu, rw_w0, rw_w2, rw_a0, rw_a2, rw_g2, rw_kk, rw_ka, rw_rk, rw_lnw, rw_lnb, fox_bf, w_up_rw, w_up_fox, w_out):
    P = dict(ln_g=ln_g, ln_b=ln_b, rw_mu=rw_mu, rw_w0=rw_w0, rw_w2=rw_w2, rw_a0=rw_a0, rw_a2=rw_a2,
             rw_g2=rw_g2, rw_kk=rw_kk, rw_ka=rw_ka, rw_rk=rw_rk, rw_lnw=rw_lnw, rw_lnb=rw_lnb, fox_bf=fox_bf)
    depth = w_ada.shape[0]
    bp, t, d = x_prompt.shape
    bs, ts, _ = x_sample.shape
    assert ts == 1
    alpha = (2 * depth) ** 0.25
    rows = 16
    c_all = jnp.zeros((rows, d), F32).at[:bp].set(c_prompt).at[bp:bp + bs].set(c_sample)
    mods = _ada_mods(c_all, w_ada, b_ada).reshape(depth, rows, N_MOD, d)
    xp = x_prompt.reshape(bp * t, d)
    xs = x_sample.reshape(bs, d)
    caches = (jnp.transpose(cache_k, (0, 1, 3, 4, 2)), jnp.transpose(cache_v, (0, 1, 3, 4, 2)),
              jnp.transpose(cache_logf, (0, 1, 3, 2)))
    outs_p, outs_s = [], []
    for l in range(depth):
        lw = _prep_layer_weights(w_ffn_up[l], w_ffn_down[l], w_in[l], w_up_rw[l], w_up_fox[l], w_out[l], 512)
        xp, st = _layer(xp, mods[l, :bp], lw, P, l, t, alpha,
                        functools.partial(_mixer_prompt, P=P, l=l, batch=bp))
        outs_p.append(st)
        xs, st = _layer(xs, mods[l, bp:bp + bs], lw, P, l, 1, alpha,
                        functools.partial(_mixer_sample, P=P, l=l, shift0=state_shift[l], state0=state_rwkv[l],
                                          caches=caches, page_table=page_table))
        outs_s.append(st)
    stack = lambda outs, i: jnp.stack([o[i] for o in outs])
    return (xp.reshape(bp, t, d), xs.reshape(bs, 1, d),
            *(stack(outs_p, i) for i in range(5)), *(stack(outs_s, i) for i in range(5)))
```

```python
import functools

import jax
import jax.numpy as jnp
from jax import lax
from jax.experimental import pallas as pl
from jax.experimental.pallas import tpu as pltpu

F32 = jnp.float32
BF16 = jnp.bfloat16

HEAD_DIM = 64
N_HEADS = 16
W_BR = N_HEADS * HEAD_DIM
LORA_W, LORA_A, LORA_G = 64, 64, 128
RW_COLS = 3 * W_BR + LORA_W + LORA_A + LORA_G
FOX_COLS = 3 * W_BR + N_HEADS
FOX_PAD = 3 * W_BR + 128
N_MOD = 9
LN_EPS = 1e-5
GN_EPS = 64e-5
LANE = 128
CHUNK = 64
SUB = 16
FFN_TF = 512
VMEM_LIMIT = 52 * 1024 * 1024


def _tile(n, pref, step=8):
    if n <= pref:
        return n
    for t in range(pref - pref % step, step - 1, -step):
        if n % t == 0:
            return t
    raise ValueError(f"no tile for {n}")


def _cparams(sem):
    return pltpu.CompilerParams(dimension_semantics=sem, vmem_limit_bytes=VMEM_LIMIT)


def _sigmoid(x):
    return 1.0 / (1.0 + jnp.exp(-x))


def _silu(x):
    return x * _sigmoid(x)


def _log_sigmoid(x):
    return -(jnp.maximum(-x, 0.0) + jnp.log1p(jnp.exp(-jnp.abs(x))))


def _layer_norm(y, g, b):
    mu = jnp.mean(y, axis=-1, keepdims=True)
    d = y - mu
    var = jnp.mean(d * d, axis=-1, keepdims=True)
    return d * lax.rsqrt(var + LN_EPS) * g + b


def _dot(a, b):
    return jnp.dot(a, b, preferred_element_type=F32)


def _ada_kernel(c_ref, w_ref, b_ref, o_ref):
    h = _silu(c_ref[...]).astype(BF16)
    o_ref[0] = _dot(h, w_ref[0].astype(BF16)) + b_ref[0]


def _ada_mods(c_all, w_ada, b_ada):
    depth, d, nm = w_ada.shape
    r = c_all.shape[0]
    tn = _tile(nm, 1024, LANE)
    return pl.pallas_call(
        _ada_kernel,
        grid=(depth, nm // tn),
        in_specs=[pl.BlockSpec((r, d), lambda l, j: (0, 0)),
                  pl.BlockSpec((1, d, tn), lambda l, j: (l, 0, j)),
                  pl.BlockSpec((1, 1, tn), lambda l, j: (l, 0, j))],
        out_specs=pl.BlockSpec((1, r, tn), lambda l, j: (l, 0, j)),
        out_shape=jax.ShapeDtypeStruct((depth, r, nm), F32),
        compiler_params=_cparams(("arbitrary", "arbitrary")),
        name="ada_mods",
    )(c_all, w_ada, b_ada.reshape(depth, 1, nm))


def _mod_spec(rows_per_batch, tm, d):
    if rows_per_batch == 1:
        return pl.BlockSpec((1, tm, d), lambda i, *_: (0, 0, 0))
    tiles = rows_per_batch // tm
    return pl.BlockSpec((1, 1, d), lambda i, *_: (i // tiles, 0, 0))


def _mod_arrays(mods, idx, rows_per_batch):
    m = mods[:, idx]
    return m[None] if rows_per_batch == 1 else m[:, None]


def _ffn_kernel(x_ref, sh_ref, sc_ref, gt_ref, wa_ref, wb_ref, wd_ref, g_ref, b_ref, o_ref,
                h_scr, acc_scr, *, alpha):
    f = pl.program_id(1)

    @pl.when(f == 0)
    def _():
        h_scr[...] = (x_ref[...] * (1.0 + sc_ref[0]) + sh_ref[0]).astype(BF16)
        acc_scr[...] = jnp.zeros_like(acc_scr)

    h = h_scr[...]
    a = _dot(h, wa_ref[...])
    b = _dot(h, wb_ref[...])
    acc_scr[...] += _dot((_silu(a) * b).astype(BF16), wd_ref[...])

    @pl.when(f == pl.num_programs(1) - 1)
    def _():
        y = alpha * x_ref[...] + 0.5 * gt_ref[0] * acc_scr[...]
        o_ref[...] = _layer_norm(y, g_ref[...], b_ref[...])


def _ffn(x, sh, sc, gt, w_up, w_down, ln_g, ln_b, rows_per_batch, alpha, tm, tf):
    m, d = x.shape
    fp = w_down.shape[0]
    nf = fp // tf
    ms = _mod_spec(rows_per_batch, tm, d)
    return pl.pallas_call(
        functools.partial(_ffn_kernel, alpha=alpha),
        grid=(m // tm, nf),
        in_specs=[pl.BlockSpec((tm, d), lambda i, f: (i, 0)), ms, ms, ms,
                  pl.BlockSpec((d, tf), lambda i, f: (0, f)),
                  pl.BlockSpec((d, tf), lambda i, f: (0, nf + f)),
                  pl.BlockSpec((tf, d), lambda i, f: (f, 0)),
                  pl.BlockSpec((1, d), lambda i, f: (0, 0)),
                  pl.BlockSpec((1, d), lambda i, f: (0, 0))],
        out_specs=pl.BlockSpec((tm, d), lambda i, f: (i, 0)),
        out_shape=jax.ShapeDtypeStruct((m, d), F32),
        scratch_shapes=[pltpu.VMEM((tm, d), BF16), pltpu.VMEM((tm, d), F32)],
        compiler_params=_cparams(("arbitrary", "arbitrary")),
        name="ffn",
    )(x, sh, sc, gt, w_up, w_up, w_down, ln_g.reshape(1, d), ln_b.reshape(1, d))


def _proj_kernel(x_ref, sh_ref, sc_ref, w_ref, o_ref, h_scr, *, act):
    @pl.when(pl.program_id(1) == 0)
    def _():
        h_scr[...] = (x_ref[...] * (1.0 + sc_ref[0]) + sh_ref[0]).astype(BF16)

    z = _dot(h_scr[...], w_ref[...])
    o_ref[...] = _sigmoid(z) if act == "sigmoid" else z


def _proj(x, sh, sc, w, rows_per_batch, tm, tn, act=None):
    m, d = x.shape
    n = w.shape[1]
    ms = _mod_spec(rows_per_batch, tm, d)
    return pl.pallas_call(
        functools.partial(_proj_kernel, act=act),
        grid=(m // tm, n // tn),
        in_specs=[pl.BlockSpec((tm, d), lambda i, j: (i, 0)), ms, ms,
                  pl.BlockSpec((d, tn), lambda i, j: (0, j))],
        out_specs=pl.BlockSpec((tm, tn), lambda i, j: (i, j)),
        out_shape=jax.ShapeDtypeStruct((m, n), F32),
        scratch_shapes=[pltpu.VMEM((tm, d), BF16)],
        compiler_params=_cparams(("arbitrary", "arbitrary")),
        name="proj",
    )(x, sh, sc, w)


def _merge_kernel(x_ref, gt_ref, orw_ref, ofox_ref, grw_ref, gfox_ref, wur_ref, wuf_ref, wo_ref,
                  g_ref, b_ref, o_ref, *, alpha):
    m = (grw_ref[...] * _dot(orw_ref[...], wur_ref[...])
         + gfox_ref[...] * _dot(ofox_ref[...], wuf_ref[...]))
    y = alpha * x_ref[...] + gt_ref[0] * _dot(m.astype(BF16), wo_ref[...])
    o_ref[...] = _layer_norm(y, g_ref[...], b_ref[...])


def _merge(x, gt, o_rw, o_fox, gates, w_up_rw, w_up_fox, w_out, ln_g, ln_b, rows_per_batch, alpha, tm):
    m, d = x.shape
    wb = o_rw.shape[1]
    ms = _mod_spec(rows_per_batch, tm, d)
    const = lambda shape: pl.BlockSpec(shape, lambda i: (0, 0), pipeline_mode=pl.Buffered(1))
    return pl.pallas_call(
        functools.partial(_merge_kernel, alpha=alpha),
        grid=(m // tm,),
        in_specs=[pl.BlockSpec((tm, d), lambda i: (i, 0)), ms,
                  pl.BlockSpec((tm, wb), lambda i: (i, 0)),
                  pl.BlockSpec((tm, wb), lambda i: (i, 0)),
                  pl.BlockSpec((tm, d), lambda i: (i, 0)),
                  pl.BlockSpec((tm, d), lambda i: (i, 1)),
                  const((wb, d)), const((wb, d)), const((d, d)), const((1, d)), const((1, d))],
        out_specs=pl.BlockSpec((tm, d), lambda i: (i, 0)),
        out_shape=jax.ShapeDtypeStruct((m, d), F32),
        compiler_params=_cparams(("arbitrary",)),
        name="merge",
    )(x, gt, o_rw, o_fox, gates, gates, w_up_rw, w_up_fox, w_out, ln_g.reshape(1, d), ln_b.reshape(1, d))


def _rw_prep_math(zs, w0, w2, a0, a2, g2):
    r = zs[:, 0:W_BR]
    k = zs[:, W_BR:2 * W_BR]
    v = zs[:, 2 * W_BR:3 * W_BR]
    c0 = 3 * W_BR
    wd = zs[:, c0:c0 + LORA_W]
    ad = zs[:, c0 + LORA_W:c0 + LORA_W + LORA_A]
    gd = zs[:, c0 + LORA_W + LORA_A:c0 + LORA_W + LORA_A + LORA_G]
    w_log = _log_sigmoid(w0 + _dot(jnp.tanh(wd).astype(BF16), w2)) - 0.5
    log_decay = -jnp.exp(w_log)
    a = _sigmoid(a0 + _dot(ad.astype(BF16), a2))
    g = _dot(_sigmoid(gd).astype(BF16), g2)
    return r, k, v, log_decay, a, g


def _rw_prep_prompt_kernel(z_ref, mu_ref, w0_ref, w2_ref, a0_ref, a2_ref, g2_ref,
                           r_ref, k_ref, v_ref, lw_ref, a_ref, g_ref, carry_scr, *, tiles_per_batch):
    i = pl.program_id(0)
    z = z_ref[...]
    tm = z.shape[0]

    @pl.when(i % tiles_per_batch == 0)
    def _():
        carry_scr[...] = jnp.zeros_like(carry_scr)

    row = lax.broadcasted_iota(jnp.int32, z.shape, 0)
    z_prev = jnp.where(row == 0, carry_scr[...], pltpu.roll(z, 1, 0))
    carry_scr[...] = z[tm - 1:tm, :]
    zs = z + (z_prev - z) * mu_ref[...]
    outs = _rw_prep_math(zs, w0_ref[...], w2_ref[...], a0_ref[...], a2_ref[...], g2_ref[...])
    for ref, val in zip((r_ref, k_ref, v_ref, lw_ref, a_ref, g_ref), outs):
        ref[...] = val


def _rw_prep_prompt(z_rw, batch, mu, w0, w2, a0, a2, g2, tm):
    m, cols = z_rw.shape
    t = m // batch
    tiles = t // tm
    vec = lambda n: pl.BlockSpec((1, n), lambda i: (0, 0))
    mat = lambda a: pl.BlockSpec(a.shape, lambda i: (0, 0))
    hm = pl.BlockSpec((tm, W_BR), lambda i: (i, 0))
    hm_shape = jax.ShapeDtypeStruct((m, W_BR), F32)
    return pl.pallas_call(
        functools.partial(_rw_prep_prompt_kernel, tiles_per_batch=tiles),
        grid=(m // tm,),
        in_specs=[pl.BlockSpec((tm, cols), lambda i: (i, 0)), vec(cols), vec(W_BR), mat(w2),
                  vec(W_BR), mat(a2), mat(g2)],
        out_specs=[hm] * 6,
        out_shape=[hm_shape] * 6,
        scratch_shapes=[pltpu.VMEM((1, cols), F32)],
        compiler_params=_cparams(("arbitrary",)),
        name="rw_prep_prompt",
    )(z_rw, mu.reshape(1, cols), w0.reshape(1, W_BR), w2, a0.reshape(1, W_BR), a2, g2)


def _rw_prep_sample_kernel(z_ref, prev_ref, mu_ref, w0_ref, w2_ref, a0_ref, a2_ref, g2_ref,
                           r_ref, k_ref, v_ref, lw_ref, a_ref, g_ref):
    z = z_ref[...]
    zs = z + (prev_ref[...] - z) * mu_ref[...]
    outs = _rw_prep_math(zs, w0_ref[...], w2_ref[...], a0_ref[...], a2_ref[...], g2_ref[...])
    for ref, val in zip((r_ref, k_ref, v_ref, lw_ref, a_ref, g_ref), outs):
        ref[...] = val


def _rw_prep_sample(z_rw, shift0, mu, w0, w2, a0, a2, g2):
    rows, cols = z_rw.shape
    full = lambda a: pl.BlockSpec(a.shape, lambda: (0,) * a.ndim)
    args = (z_rw, shift0, mu.reshape(1, cols), w0.reshape(1, W_BR), w2, a0.reshape(1, W_BR), a2, g2)
    out = jax.ShapeDtypeStruct((rows, W_BR), F32)
    return pl.pallas_call(
        _rw_prep_sample_kernel,
        in_specs=[full(a) for a in args],
        out_specs=[pl.BlockSpec((rows, W_BR), lambda: (0, 0))] * 6,
        out_shape=[out] * 6,
        compiler_params=pltpu.CompilerParams(vmem_limit_bytes=VMEM_LIMIT),
        name="rw_prep_sample",
    )(*args)


def _split_dot(x, w, passes):
    acc, rem = None, x
    for i in range(passes):
        piece = rem.astype(BF16)
        term = _dot(piece, w)
        acc = term if acc is None else acc + term
        if i + 1 < passes:
            rem = rem - piece.astype(F32)
    return acc


def _split_dot_rhs(w, x, passes):
    acc, rem = None, x
    for i in range(passes):
        piece = rem.astype(BF16)
        term = _dot(w, piece)
        acc = term if acc is None else acc + term
        if i + 1 < passes:
            rem = rem - piece.astype(F32)
    return acc


def _rwkv_chunk_kernel(r_ref, k_ref, v_ref, lw_ref, a_ref, g_ref, kkw_ref, ka_ref, rk_ref, lnw_ref,
                       lnb_ref, o_ref, s_ref, s_scr, *, pairs):
    tb = r_ref.shape[0]
    c, n, pw = CHUNK, HEAD_DIM, 2 * HEAD_DIM
    assert c == n

    @pl.when(pl.program_id(2) == 0)
    def _():
        s_scr[...] = jnp.zeros_like(s_scr)

    row = lax.broadcasted_iota(jnp.int32, (c, pw), 0)
    lane = lax.broadcasted_iota(jnp.int32, (c, pw), 1)
    first = lane < n
    col = jnp.where(first, lane, lane - n)
    lower = col <= row
    strict = col < row
    same_sub = (row // SUB) == (col // SUB)
    eye = jnp.where(row == col, 1.0, 0.0).astype(F32)
    tr = lax.broadcasted_iota(jnp.int32, (c, c), 0)
    tc = lax.broadcasted_iota(jnp.int32, (c, c), 1)
    tri = jnp.where(tc <= tr, 1.0, 0.0).astype(BF16)
    br = lax.broadcasted_iota(jnp.int32, (pw, pw), 0) // n
    bc = lax.broadcasted_iota(jnp.int32, (pw, pw), 1) // n
    same_head = br == bc
    ones_bd = jnp.where(same_head, 1.0, 0.0).astype(BF16)

    def expand(x):
        return jnp.concatenate([jnp.where(first, x, 0.0), jnp.where(first, 0.0, x)], axis=0).astype(BF16)

    def mm(x, y):
        return _dot(x.astype(BF16), expand(y))

    nt = lambda x, y: lax.dot_general(x, y, (((1,), (1,)), ((), ())), preferred_element_type=F32)

    def chunk(ci, carry):
        sl = pl.ds(pl.multiple_of(ci * c, c), c)
        ps_ = range(pairs)
        each = lambda f, *xs: [f(*args) for args in zip(*xs)]
        cols = [slice(p * pw, (p + 1) * pw) for p in ps_]
        r, k, v, lw, a, g = ([ref[sl, cs] for cs in cols] for ref in (r_ref, k_ref, v_ref, lw_ref, a_ref, g_ref))
        kkw, ka, rk, lnw, lnb = ([ref[:, cs] for cs in cols] for ref in (kkw_ref, ka_ref, rk_ref, lnw_ref, lnb_ref))
        s0 = [s_scr[p] for p in ps_]
        kk = each(lambda x, w: x * w, k, kkw)
        nrm = each(lambda x: _split_dot(x * x, ones_bd, 2), kk)
        kk = each(lambda x, n2: x / jnp.maximum(jnp.sqrt(n2), 1e-12), kk, nrm)
        k_mod = each(lambda x, ai, w: x * (1.0 + (ai - 1.0) * w), k, a, ka)
        be = each(lambda x, ai: x * ai, kk, a)
        cum = each(lambda x: _split_dot_rhs(tri, x, 3), lw)
        tot = each(lambda x: x[c - 1:c, :], cum)
        w_inv = each(lambda x: jnp.exp(-x), cum)
        w_end = each(lambda x, t: jnp.exp(t - x), cum, tot)
        ab = each(lambda x, cu, l: -x * jnp.exp(cu - l), kk, cum, lw)
        rb = each(lambda x, cu: x * jnp.exp(cu), r, cum)
        lhs = each(lambda x, y: jnp.concatenate([x, y], axis=0).astype(BF16), ab, rb)
        big = each(lambda x, b_, km, wi: nt(x, jnp.concatenate([expand(b_ * wi), expand(km * wi)], axis=0)),
                   lhs, be, k_mod, w_inv)
        from_s = each(lambda x, s: nt(x, s.astype(BF16)), lhs, s0)
        l_mat = each(lambda x: jnp.where(strict, x[:c, :pw], 0.0), big)
        m_rb = each(lambda x: jnp.where(lower, x[c:, :pw], 0.0), big)
        m_k = each(lambda x: jnp.concatenate([jnp.where(strict, x[:c, pw:], 0.0),
                                              jnp.where(lower, x[c:, pw:], 0.0)], axis=0).astype(BF16), big)
        from_v = each(lambda x, vi: _dot(x, expand(vi)), m_k, v)
        rhs = each(lambda x, y: x[:c] + y[:c], from_s, from_v)
        l_d = each(lambda x: jnp.where(same_sub, x, 0.0), l_mat)
        l_o = each(lambda x, y: x - y, l_mat, l_d)
        p2 = each(mm, l_d, l_d)
        p4 = each(mm, p2, p2)
        p8 = each(mm, p4, p4)
        t_d = each(lambda x: eye + x, l_d)
        t_d = each(lambda t, p_: t + mm(p_, t), t_d, p2)
        t_d = each(lambda t, p_: t + mm(p_, t), t_d, p4)
        t_d = each(lambda t, p_: t + mm(p_, t), t_d, p8)
        q = each(mm, t_d, l_o)
        u = each(mm, t_d, rhs)
        q2 = each(mm, q, q)
        u = each(lambda x, y: x + mm(y, x), u, q2)
        u = each(lambda x, y: x + mm(y, x), u, q)
        o = each(lambda x, y, m_, ui: x[c:] + y[c:] + mm(m_, ui), from_s, from_v, m_rb, u)
        upd = each(lambda ui, vi, b_, km, we: lax.dot_general(
            jnp.concatenate([ui, vi], axis=0).astype(BF16),
            jnp.concatenate([b_ * we, km * we], axis=0).astype(BF16),
            (((0,), (0,)), ((), ())), preferred_element_type=F32), u, v, be, k_mod, w_end)
        s_new = each(lambda s, t, x: s * jnp.exp(t) + jnp.where(same_head, x, 0.0), s0, tot, upd)
        mu = each(lambda x: _split_dot(x, ones_bd, 2) * (1.0 / n), o)
        d = each(lambda x, y: x - y, o, mu)
        var = each(lambda x: _split_dot(x * x, ones_bd, 2) * (1.0 / n), d)
        bonus = each(lambda ri, km, w, vi: _split_dot(ri * km * w, ones_bd, 2) * vi, r, k_mod, rk, v)
        out = each(lambda di, va, w, b_, bo, gi: (di * lax.rsqrt(va + GN_EPS) * w + b_ + bo) * gi,
                   d, var, lnw, lnb, bonus, g)
        for p in ps_:
            o_ref[sl, cols[p]] = out[p].astype(o_ref.dtype)
            s_scr[p] = s_new[p]
        return carry

    lax.fori_loop(0, tb // c, chunk, 0)

    @pl.when(pl.program_id(2) == pl.num_programs(2) - 1)
    def _():
        for p in range(pairs):
            s = s_scr[p]
            s_ref[0, 2 * p] = s[:n, :n]
            s_ref[0, 2 * p + 1] = s[n:, n:]


def _rwkv_prompt_scan(r, k, v, lw, a, g, kkw, ka, rk, lnw, lnb, batch, tb, pairs):
    m, w = r.shape
    t = m // batch
    nt_ = t // tb
    gw = pairs * 2 * HEAD_DIM
    seq = pl.BlockSpec((tb, gw), lambda bi, gi, ti: (bi * nt_ + ti, gi))
    par = pl.BlockSpec((1, gw), lambda bi, gi, ti: (0, gi))
    vec = lambda x: x.reshape(1, w)
    return pl.pallas_call(
        functools.partial(_rwkv_chunk_kernel, pairs=pairs),
        grid=(batch, w // gw, nt_),
        in_specs=[seq] * 6 + [par] * 5,
        out_specs=[seq, pl.BlockSpec((1, 2 * pairs, HEAD_DIM, HEAD_DIM), lambda bi, gi, ti: (bi, gi, 0, 0))],
        out_shape=[jax.ShapeDtypeStruct((m, w), BF16),
                   jax.ShapeDtypeStruct((batch, N_HEADS, HEAD_DIM, HEAD_DIM), F32)],
        scratch_shapes=[pltpu.VMEM((pairs, 2 * HEAD_DIM, 2 * HEAD_DIM), F32)],
        compiler_params=_cparams(("arbitrary", "arbitrary", "arbitrary")),
        name="rwkv_chunk",
    )(r, k, v, lw, a, g, vec(kkw), vec(ka), vec(rk), vec(lnw), vec(lnb))


def _rwkv_step_kernel(p_ref, r_ref, k_ref, lw_ref, a_ref, v_ref, g_ref, kkw_ref, ka_ref, rk_ref,
                      lnw_ref, lnb_ref, o_ref, pn_ref):
    p = p_ref[0]
    r, k, lw, a = r_ref[0], k_ref[0], lw_ref[0], a_ref[0]
    v, g = v_ref[0], g_ref[0]
    kk = k * kkw_ref[...]
    kk = kk / jnp.maximum(jnp.sqrt(jnp.sum(kk * kk, axis=1, keepdims=True)), 1e-12)
    k_mod = k * (1.0 + (a - 1.0) * ka_ref[...])
    sa = jnp.sum(p * (-kk), axis=1, keepdims=True)
    p_new = p * jnp.exp(lw) + (kk * a) * sa + k_mod * v
    pn_ref[0] = p_new
    o = jnp.sum(p_new * r, axis=1, keepdims=True)
    mu = jnp.mean(o, axis=-1, keepdims=True)
    d = o - mu
    var = jnp.mean(d * d, axis=-1, keepdims=True)
    bonus = jnp.sum(r * k_mod * rk_ref[...], axis=1, keepdims=True) * v
    o_ref[0] = (d * lax.rsqrt(var + GN_EPS) * lnw_ref[...] + lnb_ref[...] + bonus) * g


def _rwkv_sample_step(state, r, k, v, lw, a, g, kkw, ka, rk, lnw, lnb):
    rows = state.shape[0]
    h, n = N_HEADS, HEAD_DIM
    col = lambda x: x.reshape(rows, h, n, 1)
    rowv = lambda x: x.reshape(rows, h, 1, n)
    pcol = lambda x: x.reshape(h, n, 1)
    prow = lambda x: x.reshape(h, 1, n)
    cs = pl.BlockSpec((1, h, n, 1), lambda i: (i, 0, 0, 0))
    rs = pl.BlockSpec((1, h, 1, n), lambda i: (i, 0, 0, 0))
    ss = pl.BlockSpec((1, h, n, n), lambda i: (i, 0, 0, 0))
    pc = pl.BlockSpec((h, n, 1), lambda i: (0, 0, 0))
    pr = pl.BlockSpec((h, 1, n), lambda i: (0, 0, 0))
    o, p_new = pl.pallas_call(
        _rwkv_step_kernel,
        grid=(rows,),
        in_specs=[ss, cs, cs, cs, cs, rs, rs, pc, pc, pc, pr, pr],
        out_specs=[rs, ss],
        out_shape=[jax.ShapeDtypeStruct((rows, h, 1, n), F32), jax.ShapeDtypeStruct((rows, h, n, n), F32)],
        compiler_params=_cparams(("arbitrary",)),
        name="rwkv_step",
    )(jnp.swapaxes(state, 2, 3), col(r), col(k), col(lw), col(a), rowv(v), rowv(g),
      pcol(kkw), pcol(ka), pcol(rk), prow(lnw), prow(lnb))
    return o.reshape(rows, h * n), jnp.swapaxes(p_new, 2, 3)


N_BIAS = 3
AUG = 2 * HEAD_DIM
FOX_GROUP = 4


def _fox_prep_kernel(q_ref, k_ref, v_ref, zf_ref, bf_ref, qa_ref, ka_ref, vh_ref, lf_ref, kt_ref, vt_ref,
                     carry_scr, *, tiles_per_batch):
    @pl.when(pl.program_id(0) % tiles_per_batch == 0)
    def _():
        carry_scr[...] = jnp.zeros_like(carry_scr)

    lf = _log_sigmoid(zf_ref[...][:, :N_HEADS] + bf_ref[...])
    tm = lf.shape[0]
    row = lax.broadcasted_iota(jnp.int32, (tm, tm), 0)
    col = lax.broadcasted_iota(jnp.int32, (tm, tm), 1)
    cum = _split_dot_rhs(jnp.where(col <= row, 1.0, 0.0).astype(BF16), lf, 3) + carry_scr[...]
    carry_scr[...] = cum[tm - 1:tm, :]
    lf_ref[...] = lf
    pieces, rem = [], cum
    for _ in range(N_BIAS):
        piece = rem.astype(BF16).astype(F32)
        pieces.append(piece)
        rem = rem - piece
    lane = lax.broadcasted_iota(jnp.int32, (tm, HEAD_DIM), 1)
    q = q_ref[...] * HEAD_DIM ** -0.5
    k = k_ref[...]
    v = v_ref[...]
    for h in range(N_HEADS):
        hs = slice(h * HEAD_DIM, (h + 1) * HEAD_DIM)
        ext_q = jnp.where(lane < 2 * N_BIAS, 1.0, 0.0)
        ext_k = ext_q
        for i, piece in enumerate(pieces):
            f = piece[:, h:h + 1]
            ext_q = jnp.where(lane == i, f, ext_q)
            ext_k = jnp.where(lane == N_BIAS + i, -f, ext_k)
        qa_ref[h] = jnp.concatenate([q[:, hs], ext_q], axis=1).astype(BF16)
        ka_ref[h] = jnp.concatenate([k[:, hs], ext_k], axis=1).astype(BF16)
        vh_ref[h] = jnp.concatenate([v[:, hs], jnp.where(lane == 0, 1.0, 0.0)], axis=1).astype(BF16)
    for p in range(N_HEADS // 2):
        ps_ = slice(2 * p * HEAD_DIM, 2 * (p + 1) * HEAD_DIM)
        for src, dst in ((k, kt_ref), (v, vt_ref)):
            t = src[:, ps_].T
            dst[0, 2 * p] = t[:HEAD_DIM]
            dst[0, 2 * p + 1] = t[HEAD_DIM:]


def _fox_prep(z_fox, batch, bf, tm):
    m = z_fox.shape[0]
    tiles = (m // batch) // tm
    blk = lambda j: pl.BlockSpec((tm, W_BR), lambda i: (i, j))
    hm = lambda w: pl.BlockSpec((N_HEADS, tm, w), lambda i: (0, i, 0))
    tr = pl.BlockSpec((1, N_HEADS, HEAD_DIM, tm), lambda i: (i // tiles, 0, 0, i % tiles))
    tr_shape = jax.ShapeDtypeStruct((batch, N_HEADS, HEAD_DIM, m // batch), F32)
    return pl.pallas_call(
        functools.partial(_fox_prep_kernel, tiles_per_batch=tiles),
        grid=(m // tm,),
        in_specs=[blk(0), blk(1), blk(2),
                  pl.BlockSpec((tm, LANE), lambda i: (i, 3 * W_BR // LANE)),
                  pl.BlockSpec((1, N_HEADS), lambda i: (0, 0))],
        out_specs=[hm(AUG), hm(AUG), hm(AUG), pl.BlockSpec((tm, N_HEADS), lambda i: (i, 0)), tr, tr],
        out_shape=[jax.ShapeDtypeStruct((N_HEADS, m, AUG), BF16), jax.ShapeDtypeStruct((N_HEADS, m, AUG), BF16),
                   jax.ShapeDtypeStruct((N_HEADS, m, AUG), BF16), jax.ShapeDtypeStruct((m, N_HEADS), F32),
                   tr_shape, tr_shape],
        scratch_shapes=[pltpu.VMEM((1, N_HEADS), F32)],
        compiler_params=_cparams(("arbitrary",)),
        name="fox_prep",
    )(z_fox, z_fox, z_fox, z_fox, bf.reshape(1, N_HEADS))


def _fox_kernel(q_ref, k_ref, v_ref, o_ref, m_scr, acc_scr):
    i, j = pl.program_id(1), pl.program_id(2)
    tq, tk = q_ref.shape[1], k_ref.shape[1]
    nl = tk // LANE

    @pl.when(j == 0)
    def _():
        m_scr[...] = jnp.full_like(m_scr, -jnp.inf)
        acc_scr[...] = jnp.zeros_like(acc_scr)

    def accumulate(on_diagonal):
        if on_diagonal:
            causal = (lax.broadcasted_iota(jnp.int32, (tq, tk), 1)
                      <= lax.broadcasted_iota(jnp.int32, (tq, tk), 0))
        for h0 in range(0, N_HEADS, FOX_GROUP):
            hs = range(h0, h0 + FOX_GROUP)
            s = [lax.dot_general(q_ref[h], k_ref[h], (((1,), (1,)), ((), ())), preferred_element_type=F32)
                 for h in hs]
            if on_diagonal:
                s = [jnp.where(causal, x, -jnp.inf) for x in s]
            blk = []
            for x in s:
                b = x[:, :LANE]
                for c in range(1, nl):
                    b = jnp.maximum(b, x[:, c * LANE:(c + 1) * LANE])
                blk.append(b)
            m_prev = [m_scr[h] for h in hs]
            m_new = [jnp.maximum(mp, jnp.max(b, axis=-1, keepdims=True)) for mp, b in zip(m_prev, blk)]
            p = [jnp.exp(x - jnp.concatenate([mn] * nl, axis=1)).astype(BF16) for x, mn in zip(s, m_new)]
            pv = [_dot(pi, v_ref[h]) for pi, h in zip(p, hs)]
            for h, mp, mn, o in zip(hs, m_prev, m_new, pv):
                acc_scr[h] = jnp.exp(mp - mn) * acc_scr[h] + o
                m_scr[h] = mn

    @pl.when(j < i)
    def _():
        accumulate(False)

    @pl.when(j == i)
    def _():
        accumulate(True)
        for h in range(N_HEADS):
            acc = acc_scr[h]
            o_ref[:, h * HEAD_DIM:(h + 1) * HEAD_DIM] = (
                acc[:, :HEAD_DIM] / acc[:, HEAD_DIM:HEAD_DIM + 1]).astype(o_ref.dtype)


def _fox_prompt(q_aug, k_aug, v_hm, batch, tq):
    m = q_aug.shape[1]
    t = m // batch
    nq = t // tq
    kv = lambda w: pl.BlockSpec((N_HEADS, tq, w), lambda b, i, j: (0, b * nq + jnp.minimum(j, i), 0))
    return pl.pallas_call(
        _fox_kernel,
        grid=(batch, nq, nq),
        in_specs=[pl.BlockSpec((N_HEADS, tq, AUG), lambda b, i, j: (0, b * nq + i, 0)), kv(AUG), kv(AUG)],
        out_specs=pl.BlockSpec((tq, W_BR), lambda b, i, j: (b * nq + i, 0)),
        out_shape=jax.ShapeDtypeStruct((m, W_BR), BF16),
        scratch_shapes=[pltpu.VMEM((N_HEADS, tq, LANE), F32), pltpu.VMEM((N_HEADS, tq, AUG), F32)],
        compiler_params=_cparams(("arbitrary", "arbitrary", "arbitrary")),
        name="fox_prompt",
    )(q_aug, k_aug, v_hm)


PAGES_PER_STEP = 8


def _fox_paged_kernel(pt_ref, q_ref, kn_ref, vn_ref, lfn_ref, *refs):
    del pt_ref
    npg = PAGES_PER_STEP
    k_refs, v_refs, lf_refs = refs[:npg], refs[npg:2 * npg], refs[2 * npg:3 * npg]
    o_ref, m_scr, l_scr, acc_scr, suf_scr = refs[3 * npg:]
    step = pl.program_id(1)
    ps = lf_refs[0].shape[3]
    nh = N_HEADS
    scale = HEAD_DIM ** -0.5
    q = q_ref[0]
    qb = q.astype(BF16)
    head = lax.broadcasted_iota(jnp.int32, (nh, 1), 0)
    later = (lax.broadcasted_iota(jnp.int32, (ps, ps), 0) > lax.broadcasted_iota(jnp.int32, (ps, ps), 1))
    later = jnp.where(later, 1.0, 0.0).astype(BF16)

    @pl.when(step == 0)
    def _():
        m_scr[...] = jnp.sum(q * kn_ref[0], axis=-1, keepdims=True) * scale
        l_scr[...] = jnp.ones_like(l_scr)
        acc_scr[...] = vn_ref[0]
        suf_scr[...] = lfn_ref[0]

    suf = suf_scr[...]
    scores = []
    for i in range(npg):
        lft = lf_refs[i][0, 0]
        after = _split_dot(lft, later, 3) + suf
        suf = suf + jnp.sum(lft, axis=-1, keepdims=True)
        s = jnp.zeros((nh, ps), F32)
        for h in range(nh):
            sh = _dot(qb, k_refs[i][0, 0, h].astype(BF16))
            s = jnp.where(head == h, sh, s)
        scores.append(s * scale + after)
    suf_scr[...] = suf
    m_prev = m_scr[...]
    m_new = m_prev
    for s in scores:
        m_new = jnp.maximum(m_new, jnp.max(s, axis=-1, keepdims=True))
    corr = jnp.exp(m_prev - m_new)
    l_new = corr * l_scr[...]
    pv = jnp.zeros((nh, HEAD_DIM), F32)
    for i in range(npg):
        p = jnp.exp(scores[i] - m_new)
        l_new = l_new + jnp.sum(p, axis=-1, keepdims=True)
        pb = p.astype(BF16)
        for h in range(nh):
            oh = lax.dot_general(pb, v_refs[i][0, 0, h].astype(BF16), (((1,), (1,)), ((), ())),
                                 preferred_element_type=F32)
            pv = pv + jnp.where(head == h, oh, 0.0)
    l_scr[...] = l_new
    acc_scr[...] = corr * acc_scr[...] + pv
    m_scr[...] = m_new

    @pl.when(step == pl.num_programs(1) - 1)
    def _():
        o_ref[0] = (acc_scr[...] / l_scr[...]).astype(o_ref.dtype)


def _fox_paged(q, k_new, v_new, lf_new, cache_kt, cache_vt, cache_lft, layer, page_table):
    rows, n_pages = page_table.shape
    ps = cache_lft.shape[3]
    npg = PAGES_PER_STEP
    assert n_pages % npg == 0
    nh, hd = N_HEADS, HEAD_DIM
    heads = lambda x: x.reshape(rows, nh, hd)

    def page(i, nd):
        return lambda b, p, pt: (layer, pt[b, n_pages - 1 - (p * npg + i)]) + (0,) * nd

    vec = pl.BlockSpec((1, nh, hd), lambda b, p, pt: (b, 0, 0))
    grid_spec = pltpu.PrefetchScalarGridSpec(
        num_scalar_prefetch=1,
        grid=(rows, n_pages // npg),
        in_specs=([vec, vec, vec, pl.BlockSpec((1, nh, 1), lambda b, p, pt: (b, 0, 0))]
                  + [pl.BlockSpec((1, 1, nh, hd, ps), page(i, 3)) for i in range(npg)]
                  + [pl.BlockSpec((1, 1, nh, hd, ps), page(i, 3)) for i in range(npg)]
                  + [pl.BlockSpec((1, 1, nh, ps), page(i, 2)) for i in range(npg)]),
        out_specs=vec,
        scratch_shapes=[pltpu.VMEM((nh, 1), F32), pltpu.VMEM((nh, 1), F32),
                        pltpu.VMEM((nh, hd), F32), pltpu.VMEM((nh, 1), F32)],
    )
    out = pl.pallas_call(
        _fox_paged_kernel,
        grid_spec=grid_spec,
        out_shape=jax.ShapeDtypeStruct((rows, nh, hd), BF16),
        compiler_params=_cparams(("arbitrary", "arbitrary")),
        name="fox_paged",
    )(page_table, heads(q), heads(k_new), heads(v_new), lf_new.reshape(rows, nh, 1),
      *([cache_kt] * npg), *([cache_vt] * npg), *([cache_lft] * npg))
    return out.reshape(rows, nh * hd)


def _forget_sample_kernel(zf_ref, bf_ref, lf_ref):
    lf_ref[...] = _log_sigmoid(zf_ref[...][:, :N_HEADS] + bf_ref[...])


def _forget_sample(z_fox, bf):
    rows = z_fox.shape[0]
    return pl.pallas_call(
        _forget_sample_kernel,
        grid=(1,),
        in_specs=[pl.BlockSpec((rows, LANE), lambda i: (0, 3 * W_BR // LANE)),
                  pl.BlockSpec((1, N_HEADS), lambda i: (0, 0))],
        out_specs=pl.BlockSpec((rows, N_HEADS), lambda i: (0, 0)),
        out_shape=jax.ShapeDtypeStruct((rows, N_HEADS), F32),
        name="forget_sample",
    )(z_fox, bf.reshape(1, N_HEADS))


def _cast_up_kernel(w_ref, o_ref, *, n_valid):
    c = pl.program_id(2)

    @pl.when(c < n_valid)
    def _():
        o_ref[...] = w_ref[...].astype(BF16)

    @pl.when(c >= n_valid)
    def _():
        o_ref[...] = jnp.zeros_like(o_ref)


def _cast_ffn_up(w_ffn_up, fp):
    depth, nj, d, f2 = w_ffn_up.shape
    f = f2 // 2
    assert f % LANE == 0 and fp % LANE == 0
    nv, nc = f // LANE, fp // LANE
    w = w_ffn_up.reshape(depth * nj, d, f2)
    out = pl.pallas_call(
        functools.partial(_cast_up_kernel, n_valid=nv),
        grid=(depth * nj, 2, nc),
        in_specs=[pl.BlockSpec((1, d, LANE), lambda i, h, c: (i, 0, h * nv + jnp.minimum(c, nv - 1)))],
        out_specs=pl.BlockSpec((1, d, LANE), lambda i, h, c: (i, 0, h * nc + c)),
        out_shape=jax.ShapeDtypeStruct((depth * nj, d, 2 * fp), BF16),
        compiler_params=_cparams(("arbitrary", "arbitrary", "arbitrary")),
        name="cast_ffn_up",
    )(w)
    return out.reshape(depth, nj, d, 2 * fp)


def _cast_down_kernel(w_ref, o_ref, *, n_rows):
    tr = w_ref.shape[1]
    row = pl.program_id(1) * tr + lax.broadcasted_iota(jnp.int32, w_ref.shape[1:], 0)
    o_ref[0] = jnp.where(row < n_rows, w_ref[0], 0.0).astype(BF16)


def _cast_ffn_down(w_ffn_down, fp, tr):
    depth, nj, f, d = w_ffn_down.shape
    w = w_ffn_down.reshape(depth * nj, f, d)
    out = pl.pallas_call(
        functools.partial(_cast_down_kernel, n_rows=f),
        grid=(depth * nj, fp // tr),
        in_specs=[pl.BlockSpec((1, tr, d), lambda i, r: (i, r, 0))],
        out_specs=pl.BlockSpec((1, tr, d), lambda i, r: (i, r, 0)),
        out_shape=jax.ShapeDtypeStruct((depth * nj, fp, d), BF16),
        compiler_params=_cparams(("arbitrary", "arbitrary")),
        name="cast_ffn_down",
    )(w)
    return out.reshape(depth, nj, fp, d)


def _prep_layer_weights(ffn_up, ffn_down, w_in, w_up_rw, w_up_fox, w_out):
    d = w_in.shape[0]
    w_rw = w_in[:, :RW_COLS].astype(BF16)
    w_fox = jnp.pad(w_in[:, RW_COLS:RW_COLS + FOX_COLS], ((0, 0), (0, FOX_PAD - FOX_COLS))).astype(BF16)
    w_gate = w_in[:, RW_COLS + FOX_COLS:].astype(BF16)
    assert w_gate.shape[1] == 2 * d
    return dict(up=ffn_up, down=ffn_down, rw=w_rw, fox=w_fox, gate=w_gate, up_rw=w_up_rw.astype(BF16),
                up_fox=w_up_fox.astype(BF16), out=w_out.astype(BF16))


def _layer(x, mods, lw, P, l, rows_per_batch, alpha, mixer_fn):
    m, d = x.shape
    seq = m if rows_per_batch == 1 else rows_per_batch
    tm_ffn = _tile(seq, 512)
    tm = _tile(seq, 1024)
    tf = min(FFN_TF, lw["down"][0].shape[0])
    mod = lambda i: _mod_arrays(mods, i, rows_per_batch)
    x = _ffn(x, mod(0), mod(1), mod(2), lw["up"][0], lw["down"][0], P["ln_g"][l, 0], P["ln_b"][l, 0],
             rows_per_batch, alpha, tm_ffn, tf)
    sh, sc = mod(3), mod(4)
    z_rw = _proj(x, sh, sc, lw["rw"], rows_per_batch, tm, RW_COLS // 2)
    z_fox = _proj(x, sh, sc, lw["fox"], rows_per_batch, tm, FOX_PAD // 5)
    gates = _proj(x, sh, sc, lw["gate"], rows_per_batch, tm, _tile(d, 1024, LANE), act="sigmoid")
    o_rw, o_fox, st = mixer_fn(z_rw, z_fox)
    x = _merge(x, mod(5), o_rw, o_fox, gates, lw["up_rw"], lw["up_fox"], lw["out"],
               P["ln_g"][l, 1], P["ln_b"][l, 1], rows_per_batch, alpha, _tile(seq, 256))
    x = _ffn(x, mod(6), mod(7), mod(8), lw["up"][1], lw["down"][1], P["ln_g"][l, 2], P["ln_b"][l, 2],
             rows_per_batch, alpha, tm_ffn, tf)
    return x, st


def _mixer_prompt(z_rw, z_fox, P, l, batch):
    m = z_rw.shape[0]
    t = m // batch
    rw = _rw_prep_prompt(z_rw, batch, P["rw_mu"][l], P["rw_w0"][l], P["rw_w2"][l].astype(BF16),
                         P["rw_a0"][l], P["rw_a2"][l].astype(BF16), P["rw_g2"][l].astype(BF16), _tile(t, 256))
    o_rw, s_fin = _rwkv_prompt_scan(*rw, P["rw_kk"][l], P["rw_ka"][l], P["rw_rk"][l].reshape(-1),
                                    P["rw_lnw"][l], P["rw_lnb"][l], batch, _tile(t, 512, CHUNK), 8)
    q_aug, k_aug, v_hm, lf, k_t, v_t = _fox_prep(z_fox, batch, P["fox_bf"][l], _tile(t, 256))
    o_fox = _fox_prompt(q_aug, k_aug, v_hm, batch, _tile(t, 512))
    shift = z_rw.reshape(batch, t, RW_COLS)[:, -1]
    back = lambda a: jnp.transpose(a, (0, 3, 1, 2))
    st = (back(k_t), back(v_t), lf.reshape(batch, t, N_HEADS), s_fin, shift)
    return o_rw, o_fox, st


def _mixer_sample(z_rw, z_fox, P, l, shift0, state0, caches, page_table):
    rows = z_rw.shape[0]
    rw = _rw_prep_sample(z_rw, shift0, P["rw_mu"][l], P["rw_w0"][l], P["rw_w2"][l].astype(BF16),
                         P["rw_a0"][l], P["rw_a2"][l].astype(BF16), P["rw_g2"][l].astype(BF16))
    o_rw, s_new = _rwkv_sample_step(state0, *rw, P["rw_kk"][l], P["rw_ka"][l], P["rw_rk"][l].reshape(-1),
                                    P["rw_lnw"][l], P["rw_lnb"][l])
    lf = _forget_sample(z_fox, P["fox_bf"][l])
    k_new = z_fox[:, W_BR:2 * W_BR]
    v_new = z_fox[:, 2 * W_BR:3 * W_BR]
    o_fox = _fox_paged(z_fox[:, :W_BR], k_new, v_new, lf, *caches, l, page_table)
    heads = lambda a: a.reshape(rows, 1, N_HEADS, HEAD_DIM)
    st = (heads(k_new), heads(v_new), lf.reshape(rows, 1, N_HEADS), s_new, z_rw)
    return o_rw.astype(BF16), o_fox, st


def kernel(x_prompt, x_sample, c_prompt, c_sample, cache_k, cache_v, cache_logf, state_rwkv, state_shift, page_table, w_ada, b_ada, ln_g, ln_b, w_ffn_up, w_ffn_down, w_in, rw_mu, rw_w0, rw_w2, rw_a0, rw_a2, rw_g2, rw_kk, rw_ka, rw_rk, rw_lnw, rw_lnb, fox_bf, w_up_rw, w_up_fox, w_out):
    P = dict(ln_g=ln_g, ln_b=ln_b, rw_mu=rw_mu, rw_w0=rw_w0, rw_w2=rw_w2, rw_a0=rw_a0, rw_a2=rw_a2,
             rw_g2=rw_g2, rw_kk=rw_kk, rw_ka=rw_ka, rw_rk=rw_rk, rw_lnw=rw_lnw, rw_lnb=rw_lnb, fox_bf=fox_bf)
    depth = w_ada.shape[0]
    bp, t, d = x_prompt.shape
    bs, ts, _ = x_sample.shape
    assert ts == 1
    alpha = (2 * depth) ** 0.25
    rows = 16
    c_all = jnp.zeros((rows, d), F32).at[:bp].set(c_prompt).at[bp:bp + bs].set(c_sample)
    mods = _ada_mods(c_all, w_ada, b_ada).reshape(depth, rows, N_MOD, d)
    xp = x_prompt.reshape(bp * t, d)
    xs = x_sample.reshape(bs, d)
    caches = (jnp.transpose(cache_k, (0, 1, 3, 4, 2)), jnp.transpose(cache_v, (0, 1, 3, 4, 2)),
              jnp.transpose(cache_logf, (0, 1, 3, 2)))
    d_ff = w_ffn_down.shape[2]
    fp = -(-d_ff // FFN_TF) * FFN_TF
    ffn_up = _cast_ffn_up(w_ffn_up, fp)
    ffn_down = _cast_ffn_down(w_ffn_down, fp, FFN_TF)
    outs_p, outs_s = [], []
    for l in range(depth):
        lw = _prep_layer_weights(ffn_up[l], ffn_down[l], w_in[l], w_up_rw[l], w_up_fox[l], w_out[l])
        xp, st = _layer(xp, mods[l, :bp], lw, P, l, t, alpha,
                        functools.partial(_mixer_prompt, P=P, l=l, batch=bp))
        outs_p.append(st)
        xs, st = _layer(xs, mods[l, bp:bp + bs], lw, P, l, 1, alpha,
                        functools.partial(_mixer_sample, P=P, l=l, shift0=state_shift[l], state0=state_rwkv[l],
                                          caches=caches, page_table=page_table))
        outs_s.append(st)
    stack = lambda outs, i: jnp.stack([o[i] for o in outs])
    return (xp.reshape(bp, t, d), xs.reshape(bs, 1, d),
            *(stack(outs_p, i) for i in range(5)), *(stack(outs_s, i) for i in range(5)))
```

```python
import functools

import jax
import jax.numpy as jnp
from jax import lax
from jax.experimental import pallas as pl
from jax.experimental.pallas import tpu as pltpu

F32 = jnp.float32
BF16 = jnp.bfloat16

HEAD_DIM = 64
N_HEADS = 16
W_BR = N_HEADS * HEAD_DIM
LORA_W, LORA_A, LORA_G = 64, 64, 128
RW_COLS = 3 * W_BR + LORA_W + LORA_A + LORA_G
FOX_COLS = 3 * W_BR + N_HEADS
FOX_PAD = 3 * W_BR + 128
N_MOD = 9
LN_EPS = 1e-5
GN_EPS = 64e-5
LANE = 128
CHUNK = 64
SUB = 16
FFN_TF = 512
VMEM_LIMIT = 52 * 1024 * 1024


def _tile(n, pref, step=8):
    if n <= pref:
        return n
    for t in range(pref - pref % step, step - 1, -step):
        if n % t == 0:
            return t
    raise ValueError(f"no tile for {n}")


def _cparams(sem):
    return pltpu.CompilerParams(dimension_semantics=sem, vmem_limit_bytes=VMEM_LIMIT)


def _sigmoid(x):
    return 1.0 / (1.0 + jnp.exp(-x))


def _silu(x):
    return x * _sigmoid(x)


def _log_sigmoid(x):
    return -(jnp.maximum(-x, 0.0) + jnp.log1p(jnp.exp(-jnp.abs(x))))


def _layer_norm(y, g, b):
    mu = jnp.mean(y, axis=-1, keepdims=True)
    d = y - mu
    var = jnp.mean(d * d, axis=-1, keepdims=True)
    return d * lax.rsqrt(var + LN_EPS) * g + b


def _dot(a, b):
    return jnp.dot(a, b, preferred_element_type=F32)


def _ada_kernel(c_ref, w_ref, b_ref, o_ref):
    h = _silu(c_ref[...]).astype(BF16)
    o_ref[0] = _dot(h, w_ref[0].astype(BF16)) + b_ref[0]


def _ada_mods(c_all, w_ada, b_ada):
    depth, d, nm = w_ada.shape
    r = c_all.shape[0]
    tn = _tile(nm, 1024, LANE)
    return pl.pallas_call(
        _ada_kernel,
        grid=(depth, nm // tn),
        in_specs=[pl.BlockSpec((r, d), lambda l, j: (0, 0)),
                  pl.BlockSpec((1, d, tn), lambda l, j: (l, 0, j)),
                  pl.BlockSpec((1, 1, tn), lambda l, j: (l, 0, j))],
        out_specs=pl.BlockSpec((1, r, tn), lambda l, j: (l, 0, j)),
        out_shape=jax.ShapeDtypeStruct((depth, r, nm), F32),
        compiler_params=_cparams(("arbitrary", "arbitrary")),
        name="ada_mods",
    )(c_all, w_ada, b_ada.reshape(depth, 1, nm))


def _mod_spec(rows_per_batch, tm, d):
    if rows_per_batch == 1:
        return pl.BlockSpec((1, tm, d), lambda i, *_: (0, 0, 0))
    tiles = rows_per_batch // tm
    return pl.BlockSpec((1, 1, d), lambda i, *_: (i // tiles, 0, 0))


def _mod_arrays(mods, idx, rows_per_batch):
    m = mods[:, idx]
    return m[None] if rows_per_batch == 1 else m[:, None]


def _ffn_kernel(x_ref, sh_ref, sc_ref, gt_ref, wa_ref, wb_ref, wd_ref, g_ref, b_ref, o_ref,
                h_scr, acc_scr, *, alpha):
    f = pl.program_id(1)

    @pl.when(f == 0)
    def _():
        h_scr[...] = (x_ref[...] * (1.0 + sc_ref[0]) + sh_ref[0]).astype(BF16)
        acc_scr[...] = jnp.zeros_like(acc_scr)

    h = h_scr[...]
    a = _dot(h, wa_ref[...])
    b = _dot(h, wb_ref[...])
    acc_scr[...] += _dot((_silu(a) * b).astype(BF16), wd_ref[...])

    @pl.when(f == pl.num_programs(1) - 1)
    def _():
        y = alpha * x_ref[...] + 0.5 * gt_ref[0] * acc_scr[...]
        o_ref[...] = _layer_norm(y, g_ref[...], b_ref[...])


def _ffn(x, sh, sc, gt, w_up, w_down, widx, ln_g, ln_b, rows_per_batch, alpha, tm, tf):
    m, d = x.shape
    fp = w_down.shape[1]
    nf = fp // tf
    ms = _mod_spec(rows_per_batch, tm, d)
    return pl.pallas_call(
        functools.partial(_ffn_kernel, alpha=alpha),
        grid=(m // tm, nf),
        in_specs=[pl.BlockSpec((tm, d), lambda i, f: (i, 0)), ms, ms, ms,
                  pl.BlockSpec((None, d, tf), lambda i, f: (widx, 0, f)),
                  pl.BlockSpec((None, d, tf), lambda i, f: (widx, 0, nf + f)),
                  pl.BlockSpec((None, tf, d), lambda i, f: (widx, f, 0)),
                  pl.BlockSpec((1, d), lambda i, f: (0, 0)),
                  pl.BlockSpec((1, d), lambda i, f: (0, 0))],
        out_specs=pl.BlockSpec((tm, d), lambda i, f: (i, 0)),
        out_shape=jax.ShapeDtypeStruct((m, d), F32),
        scratch_shapes=[pltpu.VMEM((tm, d), BF16), pltpu.VMEM((tm, d), F32)],
        compiler_params=_cparams(("arbitrary", "arbitrary")),
        name="ffn",
    )(x, sh, sc, gt, w_up, w_up, w_down, ln_g.reshape(1, d), ln_b.reshape(1, d))


def _proj_kernel(x_ref, sh_ref, sc_ref, w_ref, o_ref, h_scr, *, act):
    @pl.when(pl.program_id(1) == 0)
    def _():
        h_scr[...] = (x_ref[...] * (1.0 + sc_ref[0]) + sh_ref[0]).astype(BF16)

    z = _dot(h_scr[...], w_ref[...])
    o_ref[...] = _sigmoid(z) if act == "sigmoid" else z


def _proj(x, sh, sc, w, rows_per_batch, tm, tn, act=None):
    m, d = x.shape
    n = w.shape[1]
    ms = _mod_spec(rows_per_batch, tm, d)
    return pl.pallas_call(
        functools.partial(_proj_kernel, act=act),
        grid=(m // tm, n // tn),
        in_specs=[pl.BlockSpec((tm, d), lambda i, j: (i, 0)), ms, ms,
                  pl.BlockSpec((d, tn), lambda i, j: (0, j))],
        out_specs=pl.BlockSpec((tm, tn), lambda i, j: (i, j)),
        out_shape=jax.ShapeDtypeStruct((m, n), F32),
        scratch_shapes=[pltpu.VMEM((tm, d), BF16)],
        compiler_params=_cparams(("arbitrary", "arbitrary")),
        name="proj",
    )(x, sh, sc, w)


def _merge_kernel(x_ref, gt_ref, orw_ref, ofox_ref, grw_ref, gfox_ref, wur_ref, wuf_ref, wo_ref,
                  g_ref, b_ref, o_ref, *, alpha):
    m = (grw_ref[...] * _dot(orw_ref[...], wur_ref[...])
         + gfox_ref[...] * _dot(ofox_ref[...], wuf_ref[...]))
    y = alpha * x_ref[...] + gt_ref[0] * _dot(m.astype(BF16), wo_ref[...])
    o_ref[...] = _layer_norm(y, g_ref[...], b_ref[...])


def _merge(x, gt, o_rw, o_fox, gates, w_up_rw, w_up_fox, w_out, ln_g, ln_b, rows_per_batch, alpha, tm):
    m, d = x.shape
    wb = o_rw.shape[1]
    ms = _mod_spec(rows_per_batch, tm, d)
    const = lambda shape: pl.BlockSpec(shape, lambda i: (0, 0), pipeline_mode=pl.Buffered(1))
    return pl.pallas_call(
        functools.partial(_merge_kernel, alpha=alpha),
        grid=(m // tm,),
        in_specs=[pl.BlockSpec((tm, d), lambda i: (i, 0)), ms,
                  pl.BlockSpec((tm, wb), lambda i: (i, 0)),
                  pl.BlockSpec((tm, wb), lambda i: (i, 0)),
                  pl.BlockSpec((tm, d), lambda i: (i, 0)),
                  pl.BlockSpec((tm, d), lambda i: (i, 1)),
                  const((wb, d)), const((wb, d)), const((d, d)), const((1, d)), const((1, d))],
        out_specs=pl.BlockSpec((tm, d), lambda i: (i, 0)),
        out_shape=jax.ShapeDtypeStruct((m, d), F32),
        compiler_params=_cparams(("arbitrary",)),
        name="merge",
    )(x, gt, o_rw, o_fox, gates, gates, w_up_rw, w_up_fox, w_out, ln_g.reshape(1, d), ln_b.reshape(1, d))


def _rw_prep_math(zs, w0, w2, a0, a2, g2):
    r = zs[:, 0:W_BR]
    k = zs[:, W_BR:2 * W_BR]
    v = zs[:, 2 * W_BR:3 * W_BR]
    c0 = 3 * W_BR
    wd = zs[:, c0:c0 + LORA_W]
    ad = zs[:, c0 + LORA_W:c0 + LORA_W + LORA_A]
    gd = zs[:, c0 + LORA_W + LORA_A:c0 + LORA_W + LORA_A + LORA_G]
    w_log = _log_sigmoid(w0 + _dot(jnp.tanh(wd).astype(BF16), w2)) - 0.5
    log_decay = -jnp.exp(w_log)
    a = _sigmoid(a0 + _dot(ad.astype(BF16), a2))
    g = _dot(_sigmoid(gd).astype(BF16), g2)
    return r, k, v, log_decay, a, g


def _rw_prep_prompt_kernel(z_ref, mu_ref, w0_ref, w2_ref, a0_ref, a2_ref, g2_ref,
                           r_ref, k_ref, v_ref, lw_ref, a_ref, g_ref, carry_scr, *, tiles_per_batch):
    i = pl.program_id(0)
    z = z_ref[...]
    tm = z.shape[0]

    @pl.when(i % tiles_per_batch == 0)
    def _():
        carry_scr[...] = jnp.zeros_like(carry_scr)

    row = lax.broadcasted_iota(jnp.int32, z.shape, 0)
    z_prev = jnp.where(row == 0, carry_scr[...], pltpu.roll(z, 1, 0))
    carry_scr[...] = z[tm - 1:tm, :]
    zs = z + (z_prev - z) * mu_ref[...]
    outs = _rw_prep_math(zs, w0_ref[...], w2_ref[...], a0_ref[...], a2_ref[...], g2_ref[...])
    for ref, val in zip((r_ref, k_ref, v_ref, lw_ref, a_ref, g_ref), outs):
        ref[...] = val


def _rw_prep_prompt(z_rw, batch, mu, w0, w2, a0, a2, g2, tm):
    m, cols = z_rw.shape
    t = m // batch
    tiles = t // tm
    vec = lambda n: pl.BlockSpec((1, n), lambda i: (0, 0))
    mat = lambda a: pl.BlockSpec(a.shape, lambda i: (0, 0))
    hm = pl.BlockSpec((tm, W_BR), lambda i: (i, 0))
    hm_shape = jax.ShapeDtypeStruct((m, W_BR), F32)
    return pl.pallas_call(
        functools.partial(_rw_prep_prompt_kernel, tiles_per_batch=tiles),
        grid=(m // tm,),
        in_specs=[pl.BlockSpec((tm, cols), lambda i: (i, 0)), vec(cols), vec(W_BR), mat(w2),
                  vec(W_BR), mat(a2), mat(g2)],
        out_specs=[hm] * 6,
        out_shape=[hm_shape] * 6,
        scratch_shapes=[pltpu.VMEM((1, cols), F32)],
        compiler_params=_cparams(("arbitrary",)),
        name="rw_prep_prompt",
    )(z_rw, mu.reshape(1, cols), w0.reshape(1, W_BR), w2, a0.reshape(1, W_BR), a2, g2)


def _rw_prep_sample_kernel(z_ref, prev_ref, mu_ref, w0_ref, w2_ref, a0_ref, a2_ref, g2_ref,
                           r_ref, k_ref, v_ref, lw_ref, a_ref, g_ref):
    z = z_ref[...]
    zs = z + (prev_ref[...] - z) * mu_ref[...]
    outs = _rw_prep_math(zs, w0_ref[...], w2_ref[...], a0_ref[...], a2_ref[...], g2_ref[...])
    for ref, val in zip((r_ref, k_ref, v_ref, lw_ref, a_ref, g_ref), outs):
        ref[...] = val


def _rw_prep_sample(z_rw, shift0, mu, w0, w2, a0, a2, g2):
    rows, cols = z_rw.shape
    full = lambda a: pl.BlockSpec(a.shape, lambda: (0,) * a.ndim)
    args = (z_rw, shift0, mu.reshape(1, cols), w0.reshape(1, W_BR), w2, a0.reshape(1, W_BR), a2, g2)
    out = jax.ShapeDtypeStruct((rows, W_BR), F32)
    return pl.pallas_call(
        _rw_prep_sample_kernel,
        in_specs=[full(a) for a in args],
        out_specs=[pl.BlockSpec((rows, W_BR), lambda: (0, 0))] * 6,
        out_shape=[out] * 6,
        compiler_params=pltpu.CompilerParams(vmem_limit_bytes=VMEM_LIMIT),
        name="rw_prep_sample",
    )(*args)


def _split_dot(x, w, passes):
    acc, rem = None, x
    for i in range(passes):
        piece = rem.astype(BF16)
        term = _dot(piece, w)
        acc = term if acc is None else acc + term
        if i + 1 < passes:
            rem = rem - piece.astype(F32)
    return acc


def _split_dot_rhs(w, x, passes):
    acc, rem = None, x
    for i in range(passes):
        piece = rem.astype(BF16)
        term = _dot(w, piece)
        acc = term if acc is None else acc + term
        if i + 1 < passes:
            rem = rem - piece.astype(F32)
    return acc


def _rwkv_chunk_kernel(r_ref, k_ref, v_ref, lw_ref, a_ref, g_ref, kkw_ref, ka_ref, rk_ref, lnw_ref,
                       lnb_ref, o_ref, s_ref, s_scr, *, pairs):
    tb = r_ref.shape[0]
    c, n, pw = CHUNK, HEAD_DIM, 2 * HEAD_DIM
    assert c == n

    @pl.when(pl.program_id(2) == 0)
    def _():
        s_scr[...] = jnp.zeros_like(s_scr)

    row = lax.broadcasted_iota(jnp.int32, (c, pw), 0)
    lane = lax.broadcasted_iota(jnp.int32, (c, pw), 1)
    first = lane < n
    col = jnp.where(first, lane, lane - n)
    lower = col <= row
    strict = col < row
    same_sub = (row // SUB) == (col // SUB)
    eye = jnp.where(row == col, 1.0, 0.0).astype(F32)
    tr = lax.broadcasted_iota(jnp.int32, (c, c), 0)
    tc = lax.broadcasted_iota(jnp.int32, (c, c), 1)
    tri = jnp.where(tc <= tr, 1.0, 0.0).astype(BF16)
    br = lax.broadcasted_iota(jnp.int32, (pw, pw), 0) // n
    bc = lax.broadcasted_iota(jnp.int32, (pw, pw), 1) // n
    same_head = br == bc
    ones_bd = jnp.where(same_head, 1.0, 0.0).astype(BF16)

    def expand(x):
        return jnp.concatenate([jnp.where(first, x, 0.0), jnp.where(first, 0.0, x)], axis=0).astype(BF16)

    def mm(x, y):
        return _dot(x.astype(BF16), expand(y))

    nt = lambda x, y: lax.dot_general(x, y, (((1,), (1,)), ((), ())), preferred_element_type=F32)

    def chunk(ci, carry):
        sl = pl.ds(pl.multiple_of(ci * c, c), c)
        ps_ = range(pairs)
        each = lambda f, *xs: [f(*args) for args in zip(*xs)]
        cols = [slice(p * pw, (p + 1) * pw) for p in ps_]
        r, k, v, lw, a, g = ([ref[sl, cs] for cs in cols] for ref in (r_ref, k_ref, v_ref, lw_ref, a_ref, g_ref))
        kkw, ka, rk, lnw, lnb = ([ref[:, cs] for cs in cols] for ref in (kkw_ref, ka_ref, rk_ref, lnw_ref, lnb_ref))
        s0 = [s_scr[p] for p in ps_]
        kk = each(lambda x, w: x * w, k, kkw)
        nrm = each(lambda x: _split_dot(x * x, ones_bd, 2), kk)
        kk = each(lambda x, n2: x / jnp.maximum(jnp.sqrt(n2), 1e-12), kk, nrm)
        k_mod = each(lambda x, ai, w: x * (1.0 + (ai - 1.0) * w), k, a, ka)
        be = each(lambda x, ai: x * ai, kk, a)
        cum = each(lambda x: _split_dot_rhs(tri, x, 3), lw)
        tot = each(lambda x: x[c - 1:c, :], cum)
        w_inv = each(lambda x: jnp.exp(-x), cum)
        w_end = each(lambda x, t: jnp.exp(t - x), cum, tot)
        ab = each(lambda x, cu, l: -x * jnp.exp(cu - l), kk, cum, lw)
        rb = each(lambda x, cu: x * jnp.exp(cu), r, cum)
        lhs = each(lambda x, y: jnp.concatenate([x, y], axis=0).astype(BF16), ab, rb)
        big = each(lambda x, b_, km, wi: nt(x, jnp.concatenate([expand(b_ * wi), expand(km * wi)], axis=0)),
                   lhs, be, k_mod, w_inv)
        from_s = each(lambda x, s: nt(x, s.astype(BF16)), lhs, s0)
        l_mat = each(lambda x: jnp.where(strict, x[:c, :pw], 0.0), big)
        m_rb = each(lambda x: jnp.where(lower, x[c:, :pw], 0.0), big)
        m_k = each(lambda x: jnp.concatenate([jnp.where(strict, x[:c, pw:], 0.0),
                                              jnp.where(lower, x[c:, pw:], 0.0)], axis=0).astype(BF16), big)
        from_v = each(lambda x, vi: _dot(x, expand(vi)), m_k, v)
        rhs = each(lambda x, y: x[:c] + y[:c], from_s, from_v)
        l_d = each(lambda x: jnp.where(same_sub, x, 0.0), l_mat)
        l_o = each(lambda x, y: x - y, l_mat, l_d)
        p2 = each(mm, l_d, l_d)
        p4 = each(mm, p2, p2)
        p8 = each(mm, p4, p4)
        t_d = each(lambda x: eye + x, l_d)
        t_d = each(lambda t, p_: t + mm(p_, t), t_d, p2)
        t_d = each(lambda t, p_: t + mm(p_, t), t_d, p4)
        t_d = each(lambda t, p_: t + mm(p_, t), t_d, p8)
        q = each(mm, t_d, l_o)
        u = each(mm, t_d, rhs)
        q2 = each(mm, q, q)
        u = each(lambda x, y: x + mm(y, x), u, q2)
        u = each(lambda x, y: x + mm(y, x), u, q)
        o = each(lambda x, y, m_, ui: x[c:] + y[c:] + mm(m_, ui), from_s, from_v, m_rb, u)
        upd = each(lambda ui, vi, b_, km, we: lax.dot_general(
            jnp.concatenate([ui, vi], axis=0).astype(BF16),
            jnp.concatenate([b_ * we, km * we], axis=0).astype(BF16),
            (((0,), (0,)), ((), ())), preferred_element_type=F32), u, v, be, k_mod, w_end)
        s_new = each(lambda s, t, x: s * jnp.exp(t) + jnp.where(same_head, x, 0.0), s0, tot, upd)
        mu = each(lambda x: _split_dot(x, ones_bd, 2) * (1.0 / n), o)
        d = each(lambda x, y: x - y, o, mu)
        var = each(lambda x: _split_dot(x * x, ones_bd, 2) * (1.0 / n), d)
        bonus = each(lambda ri, km, w, vi: _split_dot(ri * km * w, ones_bd, 2) * vi, r, k_mod, rk, v)
        out = each(lambda di, va, w, b_, bo, gi: (di * lax.rsqrt(va + GN_EPS) * w + b_ + bo) * gi,
                   d, var, lnw, lnb, bonus, g)
        for p in ps_:
            o_ref[sl, cols[p]] = out[p].astype(o_ref.dtype)
            s_scr[p] = s_new[p]
        return carry

    lax.fori_loop(0, tb // c, chunk, 0)

    @pl.when(pl.program_id(2) == pl.num_programs(2) - 1)
    def _():
        for p in range(pairs):
            s = s_scr[p]
            s_ref[0, 2 * p] = s[:n, :n]
            s_ref[0, 2 * p + 1] = s[n:, n:]


def _rwkv_prompt_scan(r, k, v, lw, a, g, kkw, ka, rk, lnw, lnb, batch, tb, pairs):
    m, w = r.shape
    t = m // batch
    nt_ = t // tb
    gw = pairs * 2 * HEAD_DIM
    seq = pl.BlockSpec((tb, gw), lambda bi, gi, ti: (bi * nt_ + ti, gi))
    par = pl.BlockSpec((1, gw), lambda bi, gi, ti: (0, gi))
    vec = lambda x: x.reshape(1, w)
    return pl.pallas_call(
        functools.partial(_rwkv_chunk_kernel, pairs=pairs),
        grid=(batch, w // gw, nt_),
        in_specs=[seq] * 6 + [par] * 5,
        out_specs=[seq, pl.BlockSpec((1, 2 * pairs, HEAD_DIM, HEAD_DIM), lambda bi, gi, ti: (bi, gi, 0, 0))],
        out_shape=[jax.ShapeDtypeStruct((m, w), BF16),
                   jax.ShapeDtypeStruct((batch, N_HEADS, HEAD_DIM, HEAD_DIM), F32)],
        scratch_shapes=[pltpu.VMEM((pairs, 2 * HEAD_DIM, 2 * HEAD_DIM), F32)],
        compiler_params=_cparams(("arbitrary", "arbitrary", "arbitrary")),
        name="rwkv_chunk",
    )(r, k, v, lw, a, g, vec(kkw), vec(ka), vec(rk), vec(lnw), vec(lnb))


def _rwkv_step_kernel(p_ref, r_ref, k_ref, lw_ref, a_ref, v_ref, g_ref, kkw_ref, ka_ref, rk_ref,
                      lnw_ref, lnb_ref, o_ref, pn_ref):
    p = p_ref[0]
    r, k, lw, a = r_ref[0], k_ref[0], lw_ref[0], a_ref[0]
    v, g = v_ref[0], g_ref[0]
    kk = k * kkw_ref[...]
    kk = kk / jnp.maximum(jnp.sqrt(jnp.sum(kk * kk, axis=1, keepdims=True)), 1e-12)
    k_mod = k * (1.0 + (a - 1.0) * ka_ref[...])
    sa = jnp.sum(p * (-kk), axis=1, keepdims=True)
    p_new = p * jnp.exp(lw) + (kk * a) * sa + k_mod * v
    pn_ref[0] = p_new
    o = jnp.sum(p_new * r, axis=1, keepdims=True)
    mu = jnp.mean(o, axis=-1, keepdims=True)
    d = o - mu
    var = jnp.mean(d * d, axis=-1, keepdims=True)
    bonus = jnp.sum(r * k_mod * rk_ref[...], axis=1, keepdims=True) * v
    o_ref[0] = (d * lax.rsqrt(var + GN_EPS) * lnw_ref[...] + lnb_ref[...] + bonus) * g


def _rwkv_sample_step(state, r, k, v, lw, a, g, kkw, ka, rk, lnw, lnb):
    rows = state.shape[0]
    h, n = N_HEADS, HEAD_DIM
    col = lambda x: x.reshape(rows, h, n, 1)
    rowv = lambda x: x.reshape(rows, h, 1, n)
    pcol = lambda x: x.reshape(h, n, 1)
    prow = lambda x: x.reshape(h, 1, n)
    cs = pl.BlockSpec((1, h, n, 1), lambda i: (i, 0, 0, 0))
    rs = pl.BlockSpec((1, h, 1, n), lambda i: (i, 0, 0, 0))
    ss = pl.BlockSpec((1, h, n, n), lambda i: (i, 0, 0, 0))
    pc = pl.BlockSpec((h, n, 1), lambda i: (0, 0, 0))
    pr = pl.BlockSpec((h, 1, n), lambda i: (0, 0, 0))
    o, p_new = pl.pallas_call(
        _rwkv_step_kernel,
        grid=(rows,),
        in_specs=[ss, cs, cs, cs, cs, rs, rs, pc, pc, pc, pr, pr],
        out_specs=[rs, ss],
        out_shape=[jax.ShapeDtypeStruct((rows, h, 1, n), F32), jax.ShapeDtypeStruct((rows, h, n, n), F32)],
        compiler_params=_cparams(("arbitrary",)),
        name="rwkv_step",
    )(jnp.swapaxes(state, 2, 3), col(r), col(k), col(lw), col(a), rowv(v), rowv(g),
      pcol(kkw), pcol(ka), pcol(rk), prow(lnw), prow(lnb))
    return o.reshape(rows, h * n), jnp.swapaxes(p_new, 2, 3)


N_BIAS = 3
AUG = 2 * HEAD_DIM
FOX_GROUP = 4


def _fox_prep_kernel(q_ref, k_ref, v_ref, zf_ref, bf_ref, qa_ref, ka_ref, vh_ref, lf_ref, kt_ref, vt_ref,
                     carry_scr, *, tiles_per_batch):
    @pl.when(pl.program_id(0) % tiles_per_batch == 0)
    def _():
        carry_scr[...] = jnp.zeros_like(carry_scr)

    lf = _log_sigmoid(zf_ref[...][:, :N_HEADS] + bf_ref[...])
    tm = lf.shape[0]
    row = lax.broadcasted_iota(jnp.int32, (tm, tm), 0)
    col = lax.broadcasted_iota(jnp.int32, (tm, tm), 1)
    cum = _split_dot_rhs(jnp.where(col <= row, 1.0, 0.0).astype(BF16), lf, 3) + carry_scr[...]
    carry_scr[...] = cum[tm - 1:tm, :]
    lf_ref[...] = lf
    pieces, rem = [], cum
    for _ in range(N_BIAS):
        piece = rem.astype(BF16).astype(F32)
        pieces.append(piece)
        rem = rem - piece
    lane = lax.broadcasted_iota(jnp.int32, (tm, HEAD_DIM), 1)
    q = q_ref[...] * HEAD_DIM ** -0.5
    k = k_ref[...]
    v = v_ref[...]
    for h in range(N_HEADS):
        hs = slice(h * HEAD_DIM, (h + 1) * HEAD_DIM)
        ext_q = jnp.where(lane < 2 * N_BIAS, 1.0, 0.0)
        ext_k = ext_q
        for i, piece in enumerate(pieces):
            f = piece[:, h:h + 1]
            ext_q = jnp.where(lane == i, f, ext_q)
            ext_k = jnp.where(lane == N_BIAS + i, -f, ext_k)
        qa_ref[h] = jnp.concatenate([q[:, hs], ext_q], axis=1).astype(BF16)
        ka_ref[h] = jnp.concatenate([k[:, hs], ext_k], axis=1).astype(BF16)
        vh_ref[h] = jnp.concatenate([v[:, hs], jnp.where(lane == 0, 1.0, 0.0)], axis=1).astype(BF16)
    for p in range(N_HEADS // 2):
        ps_ = slice(2 * p * HEAD_DIM, 2 * (p + 1) * HEAD_DIM)
        for src, dst in ((k, kt_ref), (v, vt_ref)):
            t = src[:, ps_].T
            dst[0, 2 * p] = t[:HEAD_DIM]
            dst[0, 2 * p + 1] = t[HEAD_DIM:]


def _fox_prep(z_fox, batch, bf, tm):
    m = z_fox.shape[0]
    tiles = (m // batch) // tm
    blk = lambda j: pl.BlockSpec((tm, W_BR), lambda i: (i, j))
    hm = lambda w: pl.BlockSpec((N_HEADS, tm, w), lambda i: (0, i, 0))
    tr = pl.BlockSpec((1, N_HEADS, HEAD_DIM, tm), lambda i: (i // tiles, 0, 0, i % tiles))
    tr_shape = jax.ShapeDtypeStruct((batch, N_HEADS, HEAD_DIM, m // batch), F32)
    return pl.pallas_call(
        functools.partial(_fox_prep_kernel, tiles_per_batch=tiles),
        grid=(m // tm,),
        in_specs=[blk(0), blk(1), blk(2),
                  pl.BlockSpec((tm, LANE), lambda i: (i, 3 * W_BR // LANE)),
                  pl.BlockSpec((1, N_HEADS), lambda i: (0, 0))],
        out_specs=[hm(AUG), hm(AUG), hm(AUG), pl.BlockSpec((tm, N_HEADS), lambda i: (i, 0)), tr, tr],
        out_shape=[jax.ShapeDtypeStruct((N_HEADS, m, AUG), BF16), jax.ShapeDtypeStruct((N_HEADS, m, AUG), BF16),
                   jax.ShapeDtypeStruct((N_HEADS, m, AUG), BF16), jax.ShapeDtypeStruct((m, N_HEADS), F32),
                   tr_shape, tr_shape],
        scratch_shapes=[pltpu.VMEM((1, N_HEADS), F32)],
        compiler_params=_cparams(("arbitrary",)),
        name="fox_prep",
    )(z_fox, z_fox, z_fox, z_fox, bf.reshape(1, N_HEADS))


def _fox_kernel(q_ref, k_ref, v_ref, o_ref, m_scr, acc_scr):
    i, j = pl.program_id(1), pl.program_id(2)
    tq, tk = q_ref.shape[1], k_ref.shape[1]
    nl = tk // LANE

    @pl.when(j == 0)
    def _():
        m_scr[...] = jnp.full_like(m_scr, -jnp.inf)
        acc_scr[...] = jnp.zeros_like(acc_scr)

    def accumulate(on_diagonal):
        if on_diagonal:
            causal = (lax.broadcasted_iota(jnp.int32, (tq, tk), 1)
                      <= lax.broadcasted_iota(jnp.int32, (tq, tk), 0))
        for h0 in range(0, N_HEADS, FOX_GROUP):
            hs = range(h0, h0 + FOX_GROUP)
            s = [lax.dot_general(q_ref[h], k_ref[h], (((1,), (1,)), ((), ())), preferred_element_type=F32)
                 for h in hs]
            if on_diagonal:
                s = [jnp.where(causal, x, -jnp.inf) for x in s]
            blk = []
            for x in s:
                b = x[:, :LANE]
                for c in range(1, nl):
                    b = jnp.maximum(b, x[:, c * LANE:(c + 1) * LANE])
                blk.append(b)
            m_prev = [m_scr[h] for h in hs]
            m_new = [jnp.maximum(mp, jnp.max(b, axis=-1, keepdims=True)) for mp, b in zip(m_prev, blk)]
            p = [jnp.exp(x - jnp.concatenate([mn] * nl, axis=1)).astype(BF16) for x, mn in zip(s, m_new)]
            pv = [_dot(pi, v_ref[h]) for pi, h in zip(p, hs)]
            for h, mp, mn, o in zip(hs, m_prev, m_new, pv):
                acc_scr[h] = jnp.exp(mp - mn) * acc_scr[h] + o
                m_scr[h] = mn

    @pl.when(j < i)
    def _():
        accumulate(False)

    @pl.when(j == i)
    def _():
        accumulate(True)
        for h in range(N_HEADS):
            acc = acc_scr[h]
            o_ref[:, h * HEAD_DIM:(h + 1) * HEAD_DIM] = (
                acc[:, :HEAD_DIM] / acc[:, HEAD_DIM:HEAD_DIM + 1]).astype(o_ref.dtype)


def _fox_prompt(q_aug, k_aug, v_hm, batch, tq):
    m = q_aug.shape[1]
    t = m // batch
    nq = t // tq
    kv = lambda w: pl.BlockSpec((N_HEADS, tq, w), lambda b, i, j: (0, b * nq + jnp.minimum(j, i), 0))
    return pl.pallas_call(
        _fox_kernel,
        grid=(batch, nq, nq),
        in_specs=[pl.BlockSpec((N_HEADS, tq, AUG), lambda b, i, j: (0, b * nq + i, 0)), kv(AUG), kv(AUG)],
        out_specs=pl.BlockSpec((tq, W_BR), lambda b, i, j: (b * nq + i, 0)),
        out_shape=jax.ShapeDtypeStruct((m, W_BR), BF16),
        scratch_shapes=[pltpu.VMEM((N_HEADS, tq, LANE), F32), pltpu.VMEM((N_HEADS, tq, AUG), F32)],
        compiler_params=_cparams(("arbitrary", "arbitrary", "arbitrary")),
        name="fox_prompt",
    )(q_aug, k_aug, v_hm)


PAGES_PER_STEP = 8


def _fox_paged_kernel(pt_ref, q_ref, kn_ref, vn_ref, lfn_ref, *refs):
    del pt_ref
    npg = PAGES_PER_STEP
    k_refs, v_refs, lf_refs = refs[:npg], refs[npg:2 * npg], refs[2 * npg:3 * npg]
    o_ref, m_scr, l_scr, acc_scr, suf_scr = refs[3 * npg:]
    step = pl.program_id(1)
    ps = lf_refs[0].shape[3]
    nh = N_HEADS
    scale = HEAD_DIM ** -0.5
    q = q_ref[0]
    qb = q.astype(BF16)
    head = lax.broadcasted_iota(jnp.int32, (nh, 1), 0)
    later = (lax.broadcasted_iota(jnp.int32, (ps, ps), 0) > lax.broadcasted_iota(jnp.int32, (ps, ps), 1))
    later = jnp.where(later, 1.0, 0.0).astype(BF16)

    @pl.when(step == 0)
    def _():
        m_scr[...] = jnp.sum(q * kn_ref[0], axis=-1, keepdims=True) * scale
        l_scr[...] = jnp.ones_like(l_scr)
        acc_scr[...] = vn_ref[0]
        suf_scr[...] = lfn_ref[0]

    suf = suf_scr[...]
    scores = []
    for i in range(npg):
        lft = lf_refs[i][0, 0]
        after = _split_dot(lft, later, 3) + suf
        suf = suf + jnp.sum(lft, axis=-1, keepdims=True)
        s = jnp.zeros((nh, ps), F32)
        for h in range(nh):
            sh = _dot(qb, k_refs[i][0, 0, h].astype(BF16))
            s = jnp.where(head == h, sh, s)
        scores.append(s * scale + after)
    suf_scr[...] = suf
    m_prev = m_scr[...]
    m_new = m_prev
    for s in scores:
        m_new = jnp.maximum(m_new, jnp.max(s, axis=-1, keepdims=True))
    corr = jnp.exp(m_prev - m_new)
    l_new = corr * l_scr[...]
    pv = jnp.zeros((nh, HEAD_DIM), F32)
    for i in range(npg):
        p = jnp.exp(scores[i] - m_new)
        l_new = l_new + jnp.sum(p, axis=-1, keepdims=True)
        pb = p.astype(BF16)
        for h in range(nh):
            oh = lax.dot_general(pb, v_refs[i][0, 0, h].astype(BF16), (((1,), (1,)), ((), ())),
                                 preferred_element_type=F32)
            pv = pv + jnp.where(head == h, oh, 0.0)
    l_scr[...] = l_new
    acc_scr[...] = corr * acc_scr[...] + pv
    m_scr[...] = m_new

    @pl.when(step == pl.num_programs(1) - 1)
    def _():
        o_ref[0] = (acc_scr[...] / l_scr[...]).astype(o_ref.dtype)


def _fox_paged(q, k_new, v_new, lf_new, cache_kt, cache_vt, cache_lft, layer, page_table):
    rows, n_pages = page_table.shape
    ps = cache_lft.shape[3]
    npg = PAGES_PER_STEP
    assert n_pages % npg == 0
    nh, hd = N_HEADS, HEAD_DIM
    heads = lambda x: x.reshape(rows, nh, hd)

    def page(i, nd):
        return lambda b, p, pt: (layer, pt[b, n_pages - 1 - (p * npg + i)]) + (0,) * nd

    vec = pl.BlockSpec((1, nh, hd), lambda b, p, pt: (b, 0, 0))
    grid_spec = pltpu.PrefetchScalarGridSpec(
        num_scalar_prefetch=1,
        grid=(rows, n_pages // npg),
        in_specs=([vec, vec, vec, pl.BlockSpec((1, nh, 1), lambda b, p, pt: (b, 0, 0))]
                  + [pl.BlockSpec((1, 1, nh, hd, ps), page(i, 3)) for i in range(npg)]
                  + [pl.BlockSpec((1, 1, nh, hd, ps), page(i, 3)) for i in range(npg)]
                  + [pl.BlockSpec((1, 1, nh, ps), page(i, 2)) for i in range(npg)]),
        out_specs=vec,
        scratch_shapes=[pltpu.VMEM((nh, 1), F32), pltpu.VMEM((nh, 1), F32),
                        pltpu.VMEM((nh, hd), F32), pltpu.VMEM((nh, 1), F32)],
    )
    out = pl.pallas_call(
        _fox_paged_kernel,
        grid_spec=grid_spec,
        out_shape=jax.ShapeDtypeStruct((rows, nh, hd), BF16),
        compiler_params=_cparams(("arbitrary", "arbitrary")),
        name="fox_paged",
    )(page_table, heads(q), heads(k_new), heads(v_new), lf_new.reshape(rows, nh, 1),
      *([cache_kt] * npg), *([cache_vt] * npg), *([cache_lft] * npg))
    return out.reshape(rows, nh * hd)


def _forget_sample_kernel(zf_ref, bf_ref, lf_ref):
    lf_ref[...] = _log_sigmoid(zf_ref[...][:, :N_HEADS] + bf_ref[...])


def _forget_sample(z_fox, bf):
    rows = z_fox.shape[0]
    return pl.pallas_call(
        _forget_sample_kernel,
        grid=(1,),
        in_specs=[pl.BlockSpec((rows, LANE), lambda i: (0, 3 * W_BR // LANE)),
                  pl.BlockSpec((1, N_HEADS), lambda i: (0, 0))],
        out_specs=pl.BlockSpec((rows, N_HEADS), lambda i: (0, 0)),
        out_shape=jax.ShapeDtypeStruct((rows, N_HEADS), F32),
        name="forget_sample",
    )(z_fox, bf.reshape(1, N_HEADS))


CAST_CHUNKS = 4


def _cast_up_kernel(*refs, n_valid):
    w_refs, o_ref = refs[:-1], refs[-1]
    c = pl.program_id(2)
    for i, w_ref in enumerate(w_refs):
        valid = c * CAST_CHUNKS + i < n_valid
        o_ref[0, :, i * LANE:(i + 1) * LANE] = jnp.where(valid, w_ref[0], 0.0).astype(BF16)


def _cast_ffn_up(w_ffn_up, fp):
    depth, nj, d, f2 = w_ffn_up.shape
    f = f2 // 2
    assert f % LANE == 0 and fp % (CAST_CHUNKS * LANE) == 0
    nv, nc = f // LANE, fp // LANE // CAST_CHUNKS
    w = w_ffn_up.reshape(depth * nj, d, f2)

    def chunk(i):
        return lambda n, h, c: (n, 0, h * nv + jnp.minimum(c * CAST_CHUNKS + i, nv - 1))

    return pl.pallas_call(
        functools.partial(_cast_up_kernel, n_valid=nv),
        grid=(depth * nj, 2, nc),
        in_specs=[pl.BlockSpec((1, d, LANE), chunk(i)) for i in range(CAST_CHUNKS)],
        out_specs=pl.BlockSpec((1, d, CAST_CHUNKS * LANE), lambda n, h, c: (n, 0, h * nc + c)),
        out_shape=jax.ShapeDtypeStruct((depth * nj, d, 2 * fp), BF16),
        compiler_params=_cparams(("arbitrary", "arbitrary", "arbitrary")),
        name="cast_ffn_up",
    )(*([w] * CAST_CHUNKS))


def _cast_down_kernel(w_ref, o_ref, *, n_rows):
    tr = w_ref.shape[1]
    row = pl.program_id(1) * tr + lax.broadcasted_iota(jnp.int32, w_ref.shape[1:], 0)
    o_ref[0] = jnp.where(row < n_rows, w_ref[0], 0.0).astype(BF16)


def _cast_ffn_down(w_ffn_down, fp, tr):
    depth, nj, f, d = w_ffn_down.shape
    w = w_ffn_down.reshape(depth * nj, f, d)
    return pl.pallas_call(
        functools.partial(_cast_down_kernel, n_rows=f),
        grid=(depth * nj, fp // tr),
        in_specs=[pl.BlockSpec((1, tr, d), lambda i, r: (i, r, 0))],
        out_specs=pl.BlockSpec((1, tr, d), lambda i, r: (i, r, 0)),
        out_shape=jax.ShapeDtypeStruct((depth * nj, fp, d), BF16),
        compiler_params=_cparams(("arbitrary", "arbitrary")),
        name="cast_ffn_down",
    )(w)


def _prep_layer_weights(w_in, w_up_rw, w_up_fox, w_out):
    d = w_in.shape[0]
    w_rw = w_in[:, :RW_COLS].astype(BF16)
    w_fox = jnp.pad(w_in[:, RW_COLS:RW_COLS + FOX_COLS], ((0, 0), (0, FOX_PAD - FOX_COLS))).astype(BF16)
    w_gate = w_in[:, RW_COLS + FOX_COLS:].astype(BF16)
    assert w_gate.shape[1] == 2 * d
    return dict(rw=w_rw, fox=w_fox, gate=w_gate, up_rw=w_up_rw.astype(BF16),
                up_fox=w_up_fox.astype(BF16), out=w_out.astype(BF16))


def _layer(x, mods, lw, ffn_w, P, l, rows_per_batch, alpha, mixer_fn):
    m, d = x.shape
    seq = m if rows_per_batch == 1 else rows_per_batch
    tm_ffn = _tile(seq, 512)
    tm = _tile(seq, 1024)
    tf = FFN_TF
    mod = lambda i: _mod_arrays(mods, i, rows_per_batch)
    x = _ffn(x, mod(0), mod(1), mod(2), *ffn_w, 2 * l, P["ln_g"][l, 0], P["ln_b"][l, 0],
             rows_per_batch, alpha, tm_ffn, tf)
    sh, sc = mod(3), mod(4)
    z_rw = _proj(x, sh, sc, lw["rw"], rows_per_batch, tm, RW_COLS // 2)
    z_fox = _proj(x, sh, sc, lw["fox"], rows_per_batch, tm, FOX_PAD // 5)
    gates = _proj(x, sh, sc, lw["gate"], rows_per_batch, tm, _tile(d, 1024, LANE), act="sigmoid")
    o_rw, o_fox, st = mixer_fn(z_rw, z_fox)
    x = _merge(x, mod(5), o_rw, o_fox, gates, lw["up_rw"], lw["up_fox"], lw["out"],
               P["ln_g"][l, 1], P["ln_b"][l, 1], rows_per_batch, alpha, _tile(seq, 256))
    x = _ffn(x, mod(6), mod(7), mod(8), *ffn_w, 2 * l + 1, P["ln_g"][l, 2], P["ln_b"][l, 2],
             rows_per_batch, alpha, tm_ffn, tf)
    return x, st


def _mixer_prompt(z_rw, z_fox, P, l, batch):
    m = z_rw.shape[0]
    t = m // batch
    rw = _rw_prep_prompt(z_rw, batch, P["rw_mu"][l], P["rw_w0"][l], P["rw_w2"][l].astype(BF16),
                         P["rw_a0"][l], P["rw_a2"][l].astype(BF16), P["rw_g2"][l].astype(BF16), _tile(t, 256))
    o_rw, s_fin = _rwkv_prompt_scan(*rw, P["rw_kk"][l], P["rw_ka"][l], P["rw_rk"][l].reshape(-1),
                                    P["rw_lnw"][l], P["rw_lnb"][l], batch, _tile(t, 512, CHUNK), 8)
    q_aug, k_aug, v_hm, lf, k_t, v_t = _fox_prep(z_fox, batch, P["fox_bf"][l], _tile(t, 256))
    o_fox = _fox_prompt(q_aug, k_aug, v_hm, batch, _tile(t, 512))
    shift = z_rw.reshape(batch, t, RW_COLS)[:, -1]
    back = lambda a: jnp.transpose(a, (0, 3, 1, 2))
    st = (back(k_t), back(v_t), lf.reshape(batch, t, N_HEADS), s_fin, shift)
    return o_rw, o_fox, st


def _mixer_sample(z_rw, z_fox, P, l, shift0, state0, caches, page_table):
    rows = z_rw.shape[0]
    rw = _rw_prep_sample(z_rw, shift0, P["rw_mu"][l], P["rw_w0"][l], P["rw_w2"][l].astype(BF16),
                         P["rw_a0"][l], P["rw_a2"][l].astype(BF16), P["rw_g2"][l].astype(BF16))
    o_rw, s_new = _rwkv_sample_step(state0, *rw, P["rw_kk"][l], P["rw_ka"][l], P["rw_rk"][l].reshape(-1),
                                    P["rw_lnw"][l], P["rw_lnb"][l])
    lf = _forget_sample(z_fox, P["fox_bf"][l])
    k_new = z_fox[:, W_BR:2 * W_BR]
    v_new = z_fox[:, 2 * W_BR:3 * W_BR]
    o_fox = _fox_paged(z_fox[:, :W_BR], k_new, v_new, lf, *caches, l, page_table)
    heads = lambda a: a.reshape(rows, 1, N_HEADS, HEAD_DIM)
    st = (heads(k_new), heads(v_new), lf.reshape(rows, 1, N_HEADS), s_new, z_rw)
    return o_rw.astype(BF16), o_fox, st


def kernel(x_prompt, x_sample, c_prompt, c_sample, cache_k, cache_v, cache_logf, state_rwkv, state_shift, page_table, w_ada, b_ada, ln_g, ln_b, w_ffn_up, w_ffn_down, w_in, rw_mu, rw_w0, rw_w2, rw_a0, rw_a2, rw_g2, rw_kk, rw_ka, rw_rk, rw_lnw, rw_lnb, fox_bf, w_up_rw, w_up_fox, w_out):
    P = dict(ln_g=ln_g, ln_b=ln_b, rw_mu=rw_mu, rw_w0=rw_w0, rw_w2=rw_w2, rw_a0=rw_a0, rw_a2=rw_a2,
             rw_g2=rw_g2, rw_kk=rw_kk, rw_ka=rw_ka, rw_rk=rw_rk, rw_lnw=rw_lnw, rw_lnb=rw_lnb, fox_bf=fox_bf)
    depth = w_ada.shape[0]
    bp, t, d = x_prompt.shape
    bs, ts, _ = x_sample.shape
    assert ts == 1
    alpha = (2 * depth) ** 0.25
    rows = 16
    c_all = jnp.zeros((rows, d), F32).at[:bp].set(c_prompt).at[bp:bp + bs].set(c_sample)
    mods = _ada_mods(c_all, w_ada, b_ada).reshape(depth, rows, N_MOD, d)
    xp = x_prompt.reshape(bp * t, d)
    xs = x_sample.reshape(bs, d)
    caches = (jnp.transpose(cache_k, (0, 1, 3, 4, 2)), jnp.transpose(cache_v, (0, 1, 3, 4, 2)),
              jnp.transpose(cache_logf, (0, 1, 3, 2)))
    d_ff = w_ffn_down.shape[2]
    fp = -(-d_ff // FFN_TF) * FFN_TF
    ffn_w = (_cast_ffn_up(w_ffn_up, fp), _cast_ffn_down(w_ffn_down, fp, FFN_TF))
    outs_p, outs_s = [], []
    for l in range(depth):
        lw = _prep_layer_weights(w_in[l], w_up_rw[l], w_up_fox[l], w_out[l])
        xp, st = _layer(xp, mods[l, :bp], lw, ffn_w, P, l, t, alpha,
                        functools.partial(_mixer_prompt, P=P, l=l, batch=bp))
        outs_p.append(st)
        xs, st = _layer(xs, mods[l, bp:bp + bs], lw, ffn_w, P, l, 1, alpha,
                        functools.partial(_mixer_sample, P=P, l=l, shift0=state_shift[l], state0=state_rwkv[l],
                                          caches=caches, page_table=page_table))
        outs_s.append(st)
    stack = lambda outs, i: jnp.stack([o[i] for o in outs])
    return (xp.reshape(bp, t, d), xs.reshape(bs, 1, d),
            *(stack(outs_p, i) for i in range(5)), *(stack(outs_s, i) for i in range(5)))
```

```python
import functools

import jax
import jax.numpy as jnp
from jax import lax
from jax.experimental import pallas as pl
from jax.experimental.pallas import tpu as pltpu

F32 = jnp.float32
BF16 = jnp.bfloat16

HEAD_DIM = 64
N_HEADS = 16
W_BR = N_HEADS * HEAD_DIM
LORA_W, LORA_A, LORA_G = 64, 64, 128
RW_COLS = 3 * W_BR + LORA_W + LORA_A + LORA_G
FOX_COLS = 3 * W_BR + N_HEADS
FOX_PAD = 3 * W_BR + 128
N_MOD = 9
LN_EPS = 1e-5
GN_EPS = 64e-5
LANE = 128
CHUNK = 64
SUB = 16
FFN_TF = 512
VMEM_LIMIT = 52 * 1024 * 1024


def _tile(n, pref, step=8):
    if n <= pref:
        return n
    for t in range(pref - pref % step, step - 1, -step):
        if n % t == 0:
            return t
    raise ValueError(f"no tile for {n}")


def _cparams(sem):
    return pltpu.CompilerParams(dimension_semantics=sem, vmem_limit_bytes=VMEM_LIMIT)


def _sigmoid(x):
    return 1.0 / (1.0 + jnp.exp(-x))


def _silu(x):
    return x * _sigmoid(x)


def _log_sigmoid(x):
    return -(jnp.maximum(-x, 0.0) + jnp.log1p(jnp.exp(-jnp.abs(x))))


def _layer_norm(y, g, b):
    mu = jnp.mean(y, axis=-1, keepdims=True)
    d = y - mu
    var = jnp.mean(d * d, axis=-1, keepdims=True)
    return d * lax.rsqrt(var + LN_EPS) * g + b


def _dot(a, b):
    return jnp.dot(a, b, preferred_element_type=F32)


def _ada_kernel(c_ref, w_ref, b_ref, o_ref):
    h = _silu(c_ref[...]).astype(BF16)
    o_ref[0] = _dot(h, w_ref[0].astype(BF16)) + b_ref[0]


def _ada_mods(c_all, w_ada, b_ada):
    depth, d, nm = w_ada.shape
    r = c_all.shape[0]
    tn = _tile(nm, 2048, LANE)
    return pl.pallas_call(
        _ada_kernel,
        grid=(depth, nm // tn),
        in_specs=[pl.BlockSpec((r, d), lambda l, j: (0, 0)),
                  pl.BlockSpec((1, d, tn), lambda l, j: (l, 0, j)),
                  pl.BlockSpec((1, 1, tn), lambda l, j: (l, 0, j))],
        out_specs=pl.BlockSpec((1, r, tn), lambda l, j: (l, 0, j)),
        out_shape=jax.ShapeDtypeStruct((depth, r, nm), F32),
        compiler_params=_cparams(("arbitrary", "arbitrary")),
        name="ada_mods",
    )(c_all, w_ada, b_ada.reshape(depth, 1, nm))


def _mod_spec(rows_per_batch, tm, d):
    if rows_per_batch == 1:
        return pl.BlockSpec((1, tm, d), lambda i, *_: (0, 0, 0))
    tiles = rows_per_batch // tm
    return pl.BlockSpec((1, 1, d), lambda i, *_: (i // tiles, 0, 0))


def _mod_arrays(mods, idx, rows_per_batch):
    m = mods[:, idx]
    return m[None] if rows_per_batch == 1 else m[:, None]


def _ffn_kernel(x_ref, sh_ref, sc_ref, gt_ref, wa_ref, wb_ref, wd_ref, g_ref, b_ref, o_ref,
                h_scr, acc_scr, *, alpha):
    f = pl.program_id(1)

    @pl.when(f == 0)
    def _():
        h_scr[...] = (x_ref[...] * (1.0 + sc_ref[0]) + sh_ref[0]).astype(BF16)
        acc_scr[...] = jnp.zeros_like(acc_scr)

    h = h_scr[...]
    a = _dot(h, wa_ref[...])
    b = _dot(h, wb_ref[...])
    acc_scr[...] += _dot((_silu(a) * b).astype(BF16), wd_ref[...])

    @pl.when(f == pl.num_programs(1) - 1)
    def _():
        y = alpha * x_ref[...] + 0.5 * gt_ref[0] * acc_scr[...]
        o_ref[...] = _layer_norm(y, g_ref[...], b_ref[...])


def _ffn(x, sh, sc, gt, w_up, w_down, widx, ln_g, ln_b, rows_per_batch, alpha, tm, tf):
    m, d = x.shape
    fp = w_down.shape[1]
    nf = fp // tf
    ms = _mod_spec(rows_per_batch, tm, d)
    return pl.pallas_call(
        functools.partial(_ffn_kernel, alpha=alpha),
        grid=(m // tm, nf),
        in_specs=[pl.BlockSpec((tm, d), lambda i, f: (i, 0)), ms, ms, ms,
                  pl.BlockSpec((None, d, tf), lambda i, f: (widx, 0, f)),
                  pl.BlockSpec((None, d, tf), lambda i, f: (widx, 0, nf + f)),
                  pl.BlockSpec((None, tf, d), lambda i, f: (widx, f, 0)),
                  pl.BlockSpec((1, d), lambda i, f: (0, 0)),
                  pl.BlockSpec((1, d), lambda i, f: (0, 0))],
        out_specs=pl.BlockSpec((tm, d), lambda i, f: (i, 0)),
        out_shape=jax.ShapeDtypeStruct((m, d), F32),
        scratch_shapes=[pltpu.VMEM((tm, d), BF16), pltpu.VMEM((tm, d), F32)],
        compiler_params=_cparams(("arbitrary", "arbitrary")),
        name="ffn",
    )(x, sh, sc, gt, w_up, w_up, w_down, ln_g.reshape(1, d), ln_b.reshape(1, d))


def _proj_kernel(x_ref, sh_ref, sc_ref, w_ref, o_ref, h_scr, *, act):
    @pl.when(pl.program_id(1) == 0)
    def _():
        h_scr[...] = (x_ref[...] * (1.0 + sc_ref[0]) + sh_ref[0]).astype(BF16)

    z = _dot(h_scr[...], w_ref[...])
    o_ref[...] = _sigmoid(z) if act == "sigmoid" else z


def _proj(x, sh, sc, w, rows_per_batch, tm, tn, act=None):
    m, d = x.shape
    n = w.shape[1]
    ms = _mod_spec(rows_per_batch, tm, d)
    return pl.pallas_call(
        functools.partial(_proj_kernel, act=act),
        grid=(m // tm, n // tn),
        in_specs=[pl.BlockSpec((tm, d), lambda i, j: (i, 0)), ms, ms,
                  pl.BlockSpec((d, tn), lambda i, j: (0, j))],
        out_specs=pl.BlockSpec((tm, tn), lambda i, j: (i, j)),
        out_shape=jax.ShapeDtypeStruct((m, n), F32),
        scratch_shapes=[pltpu.VMEM((tm, d), BF16)],
        compiler_params=_cparams(("arbitrary", "arbitrary")),
        name="proj",
    )(x, sh, sc, w)


def _merge_kernel(x_ref, gt_ref, orw_ref, ofox_ref, grw_ref, gfox_ref, wur_ref, wuf_ref, wo_ref,
                  g_ref, b_ref, o_ref, *, alpha):
    m = (grw_ref[...] * _dot(orw_ref[...], wur_ref[...])
         + gfox_ref[...] * _dot(ofox_ref[...], wuf_ref[...]))
    y = alpha * x_ref[...] + gt_ref[0] * _dot(m.astype(BF16), wo_ref[...])
    o_ref[...] = _layer_norm(y, g_ref[...], b_ref[...])


def _merge(x, gt, o_rw, o_fox, gates, w_up_rw, w_up_fox, w_out, ln_g, ln_b, rows_per_batch, alpha, tm):
    m, d = x.shape
    wb = o_rw.shape[1]
    ms = _mod_spec(rows_per_batch, tm, d)
    const = lambda shape: pl.BlockSpec(shape, lambda i: (0, 0), pipeline_mode=pl.Buffered(1))
    return pl.pallas_call(
        functools.partial(_merge_kernel, alpha=alpha),
        grid=(m // tm,),
        in_specs=[pl.BlockSpec((tm, d), lambda i: (i, 0)), ms,
                  pl.BlockSpec((tm, wb), lambda i: (i, 0)),
                  pl.BlockSpec((tm, wb), lambda i: (i, 0)),
                  pl.BlockSpec((tm, d), lambda i: (i, 0)),
                  pl.BlockSpec((tm, d), lambda i: (i, 1)),
                  const((wb, d)), const((wb, d)), const((d, d)), const((1, d)), const((1, d))],
        out_specs=pl.BlockSpec((tm, d), lambda i: (i, 0)),
        out_shape=jax.ShapeDtypeStruct((m, d), F32),
        compiler_params=_cparams(("arbitrary",)),
        name="merge",
    )(x, gt, o_rw, o_fox, gates, gates, w_up_rw, w_up_fox, w_out, ln_g.reshape(1, d), ln_b.reshape(1, d))


def _rw_prep_math(zs, w0, w2, a0, a2, g2):
    r = zs[:, 0:W_BR]
    k = zs[:, W_BR:2 * W_BR]
    v = zs[:, 2 * W_BR:3 * W_BR]
    c0 = 3 * W_BR
    wd = zs[:, c0:c0 + LORA_W]
    ad = zs[:, c0 + LORA_W:c0 + LORA_W + LORA_A]
    gd = zs[:, c0 + LORA_W + LORA_A:c0 + LORA_W + LORA_A + LORA_G]
    w_log = _log_sigmoid(w0 + _dot(jnp.tanh(wd).astype(BF16), w2)) - 0.5
    log_decay = -jnp.exp(w_log)
    a = _sigmoid(a0 + _dot(ad.astype(BF16), a2))
    g = _dot(_sigmoid(gd).astype(BF16), g2)
    return r, k, v, log_decay, a, g


def _rw_prep_prompt_kernel(z_ref, mu_ref, w0_ref, w2_ref, a0_ref, a2_ref, g2_ref,
                           r_ref, k_ref, v_ref, lw_ref, a_ref, g_ref, carry_scr, *, tiles_per_batch):
    i = pl.program_id(0)
    z = z_ref[...]
    tm = z.shape[0]

    @pl.when(i % tiles_per_batch == 0)
    def _():
        carry_scr[...] = jnp.zeros_like(carry_scr)

    row = lax.broadcasted_iota(jnp.int32, z.shape, 0)
    z_prev = jnp.where(row == 0, carry_scr[...], pltpu.roll(z, 1, 0))
    carry_scr[...] = z[tm - 1:tm, :]
    zs = z + (z_prev - z) * mu_ref[...]
    outs = _rw_prep_math(zs, w0_ref[...], w2_ref[...], a0_ref[...], a2_ref[...], g2_ref[...])
    for ref, val in zip((r_ref, k_ref, v_ref, lw_ref, a_ref, g_ref), outs):
        ref[...] = val


def _rw_prep_prompt(z_rw, batch, mu, w0, w2, a0, a2, g2, tm):
    m, cols = z_rw.shape
    t = m // batch
    tiles = t // tm
    vec = lambda n: pl.BlockSpec((1, n), lambda i: (0, 0))
    mat = lambda a: pl.BlockSpec(a.shape, lambda i: (0, 0))
    hm = pl.BlockSpec((tm, W_BR), lambda i: (i, 0))
    hm_shape = jax.ShapeDtypeStruct((m, W_BR), F32)
    return pl.pallas_call(
        functools.partial(_rw_prep_prompt_kernel, tiles_per_batch=tiles),
        grid=(m // tm,),
        in_specs=[pl.BlockSpec((tm, cols), lambda i: (i, 0)), vec(cols), vec(W_BR), mat(w2),
                  vec(W_BR), mat(a2), mat(g2)],
        out_specs=[hm] * 6,
        out_shape=[hm_shape] * 6,
        scratch_shapes=[pltpu.VMEM((1, cols), F32)],
        compiler_params=_cparams(("arbitrary",)),
        name="rw_prep_prompt",
    )(z_rw, mu.reshape(1, cols), w0.reshape(1, W_BR), w2, a0.reshape(1, W_BR), a2, g2)


def _rw_prep_sample_kernel(z_ref, prev_ref, mu_ref, w0_ref, w2_ref, a0_ref, a2_ref, g2_ref,
                           r_ref, k_ref, v_ref, lw_ref, a_ref, g_ref):
    z = z_ref[...]
    zs = z + (prev_ref[...] - z) * mu_ref[...]
    outs = _rw_prep_math(zs, w0_ref[...], w2_ref[...], a0_ref[...], a2_ref[...], g2_ref[...])
    for ref, val in zip((r_ref, k_ref, v_ref, lw_ref, a_ref, g_ref), outs):
        ref[...] = val


def _rw_prep_sample(z_rw, shift0, mu, w0, w2, a0, a2, g2):
    rows, cols = z_rw.shape
    full = lambda a: pl.BlockSpec(a.shape, lambda: (0,) * a.ndim)
    args = (z_rw, shift0, mu.reshape(1, cols), w0.reshape(1, W_BR), w2, a0.reshape(1, W_BR), a2, g2)
    out = jax.ShapeDtypeStruct((rows, W_BR), F32)
    return pl.pallas_call(
        _rw_prep_sample_kernel,
        in_specs=[full(a) for a in args],
        out_specs=[pl.BlockSpec((rows, W_BR), lambda: (0, 0))] * 6,
        out_shape=[out] * 6,
        compiler_params=pltpu.CompilerParams(vmem_limit_bytes=VMEM_LIMIT),
        name="rw_prep_sample",
    )(*args)


def _split_dot(x, w, passes):
    acc, rem = None, x
    for i in range(passes):
        piece = rem.astype(BF16)
        term = _dot(piece, w)
        acc = term if acc is None else acc + term
        if i + 1 < passes:
            rem = rem - piece.astype(F32)
    return acc


def _split_dot_rhs(w, x, passes):
    acc, rem = None, x
    for i in range(passes):
        piece = rem.astype(BF16)
        term = _dot(w, piece)
        acc = term if acc is None else acc + term
        if i + 1 < passes:
            rem = rem - piece.astype(F32)
    return acc


def _rwkv_chunk_kernel(r_ref, k_ref, v_ref, lw_ref, a_ref, g_ref, kkw_ref, ka_ref, rk_ref, lnw_ref,
                       lnb_ref, o_ref, s_ref, s_scr, *, pairs):
    tb = r_ref.shape[0]
    c, n, pw = CHUNK, HEAD_DIM, 2 * HEAD_DIM
    assert c == n

    @pl.when(pl.program_id(2) == 0)
    def _():
        s_scr[...] = jnp.zeros_like(s_scr)

    row = lax.broadcasted_iota(jnp.int32, (c, pw), 0)
    lane = lax.broadcasted_iota(jnp.int32, (c, pw), 1)
    first = lane < n
    col = jnp.where(first, lane, lane - n)
    lower = col <= row
    strict = col < row
    same_sub = (row // SUB) == (col // SUB)
    eye = jnp.where(row == col, 1.0, 0.0).astype(F32)
    tr = lax.broadcasted_iota(jnp.int32, (c, c), 0)
    tc = lax.broadcasted_iota(jnp.int32, (c, c), 1)
    tri = jnp.where(tc <= tr, 1.0, 0.0).astype(BF16)
    br = lax.broadcasted_iota(jnp.int32, (pw, pw), 0) // n
    bc = lax.broadcasted_iota(jnp.int32, (pw, pw), 1) // n
    same_head = br == bc
    ones_bd = jnp.where(same_head, 1.0, 0.0).astype(BF16)

    def expand(x):
        return jnp.concatenate([jnp.where(first, x, 0.0), jnp.where(first, 0.0, x)], axis=0).astype(BF16)

    def mm(x, y):
        return _dot(x.astype(BF16), expand(y))

    nt = lambda x, y: lax.dot_general(x, y, (((1,), (1,)), ((), ())), preferred_element_type=F32)

    def chunk(ci, carry):
        sl = pl.ds(pl.multiple_of(ci * c, c), c)
        ps_ = range(pairs)
        each = lambda f, *xs: [f(*args) for args in zip(*xs)]
        cols = [slice(p * pw, (p + 1) * pw) for p in ps_]
        r, k, v, lw, a, g = ([ref[sl, cs] for cs in cols] for ref in (r_ref, k_ref, v_ref, lw_ref, a_ref, g_ref))
        kkw, ka, rk, lnw, lnb = ([ref[:, cs] for cs in cols] for ref in (kkw_ref, ka_ref, rk_ref, lnw_ref, lnb_ref))
        s0 = [s_scr[p] for p in ps_]
        kk = each(lambda x, w: x * w, k, kkw)
        nrm = each(lambda x: _split_dot(x * x, ones_bd, 2), kk)
        kk = each(lambda x, n2: x / jnp.maximum(jnp.sqrt(n2), 1e-12), kk, nrm)
        k_mod = each(lambda x, ai, w: x * (1.0 + (ai - 1.0) * w), k, a, ka)
        be = each(lambda x, ai: x * ai, kk, a)
        cum = each(lambda x: _split_dot_rhs(tri, x, 3), lw)
        tot = each(lambda x: x[c - 1:c, :], cum)
        w_inv = each(lambda x: jnp.exp(-x), cum)
        w_end = each(lambda x, t: jnp.exp(t - x), cum, tot)
        ab = each(lambda x, cu, l: -x * jnp.exp(cu - l), kk, cum, lw)
        rb = each(lambda x, cu: x * jnp.exp(cu), r, cum)
        lhs = each(lambda x, y: jnp.concatenate([x, y], axis=0).astype(BF16), ab, rb)
        big = each(lambda x, b_, km, wi: nt(x, jnp.concatenate([expand(b_ * wi), expand(km * wi)], axis=0)),
                   lhs, be, k_mod, w_inv)
        from_s = each(lambda x, s: nt(x, s.astype(BF16)), lhs, s0)
        l_mat = each(lambda x: jnp.where(strict, x[:c, :pw], 0.0), big)
        m_rb = each(lambda x: jnp.where(lower, x[c:, :pw], 0.0), big)
        m_k = each(lambda x: jnp.concatenate([jnp.where(strict, x[:c, pw:], 0.0),
                                              jnp.where(lower, x[c:, pw:], 0.0)], axis=0).astype(BF16), big)
        from_v = each(lambda x, vi: _dot(x, expand(vi)), m_k, v)
        rhs = each(lambda x, y: x[:c] + y[:c], from_s, from_v)
        l_d = each(lambda x: jnp.where(same_sub, x, 0.0), l_mat)
        l_o = each(lambda x, y: x - y, l_mat, l_d)
        p2 = each(mm, l_d, l_d)
        p4 = each(mm, p2, p2)
        p8 = each(mm, p4, p4)
        t_d = each(lambda x: eye + x, l_d)
        t_d = each(lambda t, p_: t + mm(p_, t), t_d, p2)
        t_d = each(lambda t, p_: t + mm(p_, t), t_d, p4)
        t_d = each(lambda t, p_: t + mm(p_, t), t_d, p8)
        q = each(mm, t_d, l_o)
        u = each(mm, t_d, rhs)
        q2 = each(mm, q, q)
        u = each(lambda x, y: x + mm(y, x), u, q2)
        u = each(lambda x, y: x + mm(y, x), u, q)
        o = each(lambda x, y, m_, ui: x[c:] + y[c:] + mm(m_, ui), from_s, from_v, m_rb, u)
        upd = each(lambda ui, vi, b_, km, we: lax.dot_general(
            jnp.concatenate([ui, vi], axis=0).astype(BF16),
            jnp.concatenate([b_ * we, km * we], axis=0).astype(BF16),
            (((0,), (0,)), ((), ())), preferred_element_type=F32), u, v, be, k_mod, w_end)
        s_new = each(lambda s, t, x: s * jnp.exp(t) + jnp.where(same_head, x, 0.0), s0, tot, upd)
        mu = each(lambda x: _split_dot(x, ones_bd, 2) * (1.0 / n), o)
        d = each(lambda x, y: x - y, o, mu)
        var = each(lambda x: _split_dot(x * x, ones_bd, 2) * (1.0 / n), d)
        bonus = each(lambda ri, km, w, vi: _split_dot(ri * km * w, ones_bd, 2) * vi, r, k_mod, rk, v)
        out = each(lambda di, va, w, b_, bo, gi: (di * lax.rsqrt(va + GN_EPS) * w + b_ + bo) * gi,
                   d, var, lnw, lnb, bonus, g)
        for p in ps_:
            o_ref[sl, cols[p]] = out[p].astype(o_ref.dtype)
            s_scr[p] = s_new[p]
        return carry

    lax.fori_loop(0, tb // c, chunk, 0)

    @pl.when(pl.program_id(2) == pl.num_programs(2) - 1)
    def _():
        for p in range(pairs):
            s = s_scr[p]
            s_ref[0, 2 * p] = s[:n, :n]
            s_ref[0, 2 * p + 1] = s[n:, n:]


def _rwkv_prompt_scan(r, k, v, lw, a, g, kkw, ka, rk, lnw, lnb, batch, tb, pairs):
    m, w = r.shape
    t = m // batch
    nt_ = t // tb
    gw = pairs * 2 * HEAD_DIM
    seq = pl.BlockSpec((tb, gw), lambda bi, gi, ti: (bi * nt_ + ti, gi))
    par = pl.BlockSpec((1, gw), lambda bi, gi, ti: (0, gi))
    vec = lambda x: x.reshape(1, w)
    return pl.pallas_call(
        functools.partial(_rwkv_chunk_kernel, pairs=pairs),
        grid=(batch, w // gw, nt_),
        in_specs=[seq] * 6 + [par] * 5,
        out_specs=[seq, pl.BlockSpec((1, 2 * pairs, HEAD_DIM, HEAD_DIM), lambda bi, gi, ti: (bi, gi, 0, 0))],
        out_shape=[jax.ShapeDtypeStruct((m, w), BF16),
                   jax.ShapeDtypeStruct((batch, N_HEADS, HEAD_DIM, HEAD_DIM), F32)],
        scratch_shapes=[pltpu.VMEM((pairs, 2 * HEAD_DIM, 2 * HEAD_DIM), F32)],
        compiler_params=_cparams(("arbitrary", "arbitrary", "arbitrary")),
        name="rwkv_chunk",
    )(r, k, v, lw, a, g, vec(kkw), vec(ka), vec(rk), vec(lnw), vec(lnb))


def _rwkv_step_kernel(p_ref, r_ref, k_ref, lw_ref, a_ref, v_ref, g_ref, kkw_ref, ka_ref, rk_ref,
                      lnw_ref, lnb_ref, o_ref, pn_ref):
    p = p_ref[0]
    r, k, lw, a = r_ref[0], k_ref[0], lw_ref[0], a_ref[0]
    v, g = v_ref[0], g_ref[0]
    kk = k * kkw_ref[...]
    kk = kk / jnp.maximum(jnp.sqrt(jnp.sum(kk * kk, axis=1, keepdims=True)), 1e-12)
    k_mod = k * (1.0 + (a - 1.0) * ka_ref[...])
    sa = jnp.sum(p * (-kk), axis=1, keepdims=True)
    p_new = p * jnp.exp(lw) + (kk * a) * sa + k_mod * v
    pn_ref[0] = p_new
    o = jnp.sum(p_new * r, axis=1, keepdims=True)
    mu = jnp.mean(o, axis=-1, keepdims=True)
    d = o - mu
    var = jnp.mean(d * d, axis=-1, keepdims=True)
    bonus = jnp.sum(r * k_mod * rk_ref[...], axis=1, keepdims=True) * v
    o_ref[0] = (d * lax.rsqrt(var + GN_EPS) * lnw_ref[...] + lnb_ref[...] + bonus) * g


def _rwkv_sample_step(state, r, k, v, lw, a, g, kkw, ka, rk, lnw, lnb):
    rows = state.shape[0]
    h, n = N_HEADS, HEAD_DIM
    col = lambda x: x.reshape(rows, h, n, 1)
    rowv = lambda x: x.reshape(rows, h, 1, n)
    pcol = lambda x: x.reshape(h, n, 1)
    prow = lambda x: x.reshape(h, 1, n)
    cs = pl.BlockSpec((1, h, n, 1), lambda i: (i, 0, 0, 0))
    rs = pl.BlockSpec((1, h, 1, n), lambda i: (i, 0, 0, 0))
    ss = pl.BlockSpec((1, h, n, n), lambda i: (i, 0, 0, 0))
    pc = pl.BlockSpec((h, n, 1), lambda i: (0, 0, 0))
    pr = pl.BlockSpec((h, 1, n), lambda i: (0, 0, 0))
    o, p_new = pl.pallas_call(
        _rwkv_step_kernel,
        grid=(rows,),
        in_specs=[ss, cs, cs, cs, cs, rs, rs, pc, pc, pc, pr, pr],
        out_specs=[rs, ss],
        out_shape=[jax.ShapeDtypeStruct((rows, h, 1, n), F32), jax.ShapeDtypeStruct((rows, h, n, n), F32)],
        compiler_params=_cparams(("arbitrary",)),
        name="rwkv_step",
    )(jnp.swapaxes(state, 2, 3), col(r), col(k), col(lw), col(a), rowv(v), rowv(g),
      pcol(kkw), pcol(ka), pcol(rk), prow(lnw), prow(lnb))
    return o.reshape(rows, h * n), jnp.swapaxes(p_new, 2, 3)


N_BIAS = 3
AUG = 2 * HEAD_DIM
FOX_GROUP = 8


def _fox_prep_kernel(q_ref, k_ref, v_ref, zf_ref, bf_ref, qa_ref, ka_ref, vh_ref, lf_ref, kt_ref, vt_ref,
                     carry_scr, *, tiles_per_batch):
    @pl.when(pl.program_id(0) % tiles_per_batch == 0)
    def _():
        carry_scr[...] = jnp.zeros_like(carry_scr)

    lf = _log_sigmoid(zf_ref[...][:, :N_HEADS] + bf_ref[...])
    tm = lf.shape[0]
    row = lax.broadcasted_iota(jnp.int32, (tm, tm), 0)
    col = lax.broadcasted_iota(jnp.int32, (tm, tm), 1)
    cum = _split_dot_rhs(jnp.where(col <= row, 1.0, 0.0).astype(BF16), lf, 3) + carry_scr[...]
    carry_scr[...] = cum[tm - 1:tm, :]
    lf_ref[...] = lf
    pieces, rem = [], cum
    for _ in range(N_BIAS):
        piece = rem.astype(BF16).astype(F32)
        pieces.append(piece)
        rem = rem - piece
    lane = lax.broadcasted_iota(jnp.int32, (tm, HEAD_DIM), 1)
    q = q_ref[...] * HEAD_DIM ** -0.5
    k = k_ref[...]
    v = v_ref[...]
    for h in range(N_HEADS):
        hs = slice(h * HEAD_DIM, (h + 1) * HEAD_DIM)
        ext_q = jnp.where(lane < 2 * N_BIAS, 1.0, 0.0)
        ext_k = ext_q
        for i, piece in enumerate(pieces):
            f = piece[:, h:h + 1]
            ext_q = jnp.where(lane == i, f, ext_q)
            ext_k = jnp.where(lane == N_BIAS + i, -f, ext_k)
        qa_ref[h] = jnp.concatenate([q[:, hs], ext_q], axis=1).astype(BF16)
        ka_ref[h] = jnp.concatenate([k[:, hs], ext_k], axis=1).astype(BF16)
        vh_ref[h] = jnp.concatenate([v[:, hs], jnp.where(lane == 0, 1.0, 0.0)], axis=1).astype(BF16)
    for p in range(N_HEADS // 2):
        ps_ = slice(2 * p * HEAD_DIM, 2 * (p + 1) * HEAD_DIM)
        for src, dst in ((k, kt_ref), (v, vt_ref)):
            t = src[:, ps_].T
            dst[0, 2 * p] = t[:HEAD_DIM]
            dst[0, 2 * p + 1] = t[HEAD_DIM:]


def _fox_prep(z_fox, batch, bf, tm):
    m = z_fox.shape[0]
    tiles = (m // batch) // tm
    blk = lambda j: pl.BlockSpec((tm, W_BR), lambda i: (i, j))
    hm = lambda w: pl.BlockSpec((N_HEADS, tm, w), lambda i: (0, i, 0))
    tr = pl.BlockSpec((1, N_HEADS, HEAD_DIM, tm), lambda i: (i // tiles, 0, 0, i % tiles))
    tr_shape = jax.ShapeDtypeStruct((batch, N_HEADS, HEAD_DIM, m // batch), F32)
    return pl.pallas_call(
        functools.partial(_fox_prep_kernel, tiles_per_batch=tiles),
        grid=(m // tm,),
        in_specs=[blk(0), blk(1), blk(2),
                  pl.BlockSpec((tm, LANE), lambda i: (i, 3 * W_BR // LANE)),
                  pl.BlockSpec((1, N_HEADS), lambda i: (0, 0))],
        out_specs=[hm(AUG), hm(AUG), hm(AUG), pl.BlockSpec((tm, N_HEADS), lambda i: (i, 0)), tr, tr],
        out_shape=[jax.ShapeDtypeStruct((N_HEADS, m, AUG), BF16), jax.ShapeDtypeStruct((N_HEADS, m, AUG), BF16),
                   jax.ShapeDtypeStruct((N_HEADS, m, AUG), BF16), jax.ShapeDtypeStruct((m, N_HEADS), F32),
                   tr_shape, tr_shape],
        scratch_shapes=[pltpu.VMEM((1, N_HEADS), F32)],
        compiler_params=_cparams(("arbitrary",)),
        name="fox_prep",
    )(z_fox, z_fox, z_fox, z_fox, bf.reshape(1, N_HEADS))


def _fox_kernel(q_ref, k_ref, v_ref, o_ref, m_scr, acc_scr):
    i, j = pl.program_id(1), pl.program_id(2)
    tq, tk = q_ref.shape[1], k_ref.shape[1]
    nl = tk // LANE

    @pl.when(j == 0)
    def _():
        m_scr[...] = jnp.full_like(m_scr, -jnp.inf)
        acc_scr[...] = jnp.zeros_like(acc_scr)

    def accumulate(on_diagonal):
        if on_diagonal:
            causal = (lax.broadcasted_iota(jnp.int32, (tq, tk), 1)
                      <= lax.broadcasted_iota(jnp.int32, (tq, tk), 0))
        for h0 in range(0, N_HEADS, FOX_GROUP):
            hs = range(h0, h0 + FOX_GROUP)
            s = [lax.dot_general(q_ref[h], k_ref[h], (((1,), (1,)), ((), ())), preferred_element_type=F32)
                 for h in hs]
            if on_diagonal:
                s = [jnp.where(causal, x, -jnp.inf) for x in s]
            blk = []
            for x in s:
                b = x[:, :LANE]
                for c in range(1, nl):
                    b = jnp.maximum(b, x[:, c * LANE:(c + 1) * LANE])
                blk.append(b)
            m_prev = [m_scr[h] for h in hs]
            m_new = [jnp.maximum(mp, jnp.max(b, axis=-1, keepdims=True)) for mp, b in zip(m_prev, blk)]
            p = [jnp.exp(x - jnp.concatenate([mn] * nl, axis=1)).astype(BF16) for x, mn in zip(s, m_new)]
            pv = [_dot(pi, v_ref[h]) for pi, h in zip(p, hs)]
            for h, mp, mn, o in zip(hs, m_prev, m_new, pv):
                acc_scr[h] = jnp.exp(mp - mn) * acc_scr[h] + o
                m_scr[h] = mn

    @pl.when(j < i)
    def _():
        accumulate(False)

    @pl.when(j == i)
    def _():
        accumulate(True)
        for h in range(N_HEADS):
            acc = acc_scr[h]
            o_ref[:, h * HEAD_DIM:(h + 1) * HEAD_DIM] = (
                acc[:, :HEAD_DIM] / acc[:, HEAD_DIM:HEAD_DIM + 1]).astype(o_ref.dtype)


def _fox_prompt(q_aug, k_aug, v_hm, batch, tq):
    m = q_aug.shape[1]
    t = m // batch
    nq = t // tq
    kv = lambda w: pl.BlockSpec((N_HEADS, tq, w), lambda b, i, j: (0, b * nq + jnp.minimum(j, i), 0))
    return pl.pallas_call(
        _fox_kernel,
        grid=(batch, nq, nq),
        in_specs=[pl.BlockSpec((N_HEADS, tq, AUG), lambda b, i, j: (0, b * nq + i, 0)), kv(AUG), kv(AUG)],
        out_specs=pl.BlockSpec((tq, W_BR), lambda b, i, j: (b * nq + i, 0)),
        out_shape=jax.ShapeDtypeStruct((m, W_BR), BF16),
        scratch_shapes=[pltpu.VMEM((N_HEADS, tq, LANE), F32), pltpu.VMEM((N_HEADS, tq, AUG), F32)],
        compiler_params=_cparams(("arbitrary", "arbitrary", "arbitrary")),
        name="fox_prompt",
    )(q_aug, k_aug, v_hm)


PAGES_PER_STEP = 16


def _fox_paged_kernel(pt_ref, q_ref, kn_ref, vn_ref, lfn_ref, *refs):
    del pt_ref
    npg = PAGES_PER_STEP
    k_refs, v_refs, lf_refs = refs[:npg], refs[npg:2 * npg], refs[2 * npg:3 * npg]
    o_ref, m_scr, l_scr, acc_scr, suf_scr = refs[3 * npg:]
    step = pl.program_id(1)
    ps = lf_refs[0].shape[3]
    nh = N_HEADS
    scale = HEAD_DIM ** -0.5
    q = q_ref[0]
    qb = q.astype(BF16)
    head = lax.broadcasted_iota(jnp.int32, (nh, 1), 0)
    later = (lax.broadcasted_iota(jnp.int32, (ps, ps), 0) > lax.broadcasted_iota(jnp.int32, (ps, ps), 1))
    later = jnp.where(later, 1.0, 0.0).astype(BF16)

    @pl.when(step == 0)
    def _():
        m_scr[...] = jnp.sum(q * kn_ref[0], axis=-1, keepdims=True) * scale
        l_scr[...] = jnp.ones_like(l_scr)
        acc_scr[...] = vn_ref[0]
        suf_scr[...] = lfn_ref[0]

    suf = suf_scr[...]
    scores = []
    for i in range(npg):
        lft = lf_refs[i][0, 0]
        after = _split_dot(lft, later, 3) + suf
        suf = suf + jnp.sum(lft, axis=-1, keepdims=True)
        s = jnp.zeros((nh, ps), F32)
        for h in range(nh):
            sh = _dot(qb, k_refs[i][0, 0, h].astype(BF16))
            s = jnp.where(head == h, sh, s)
        scores.append(s * scale + after)
    suf_scr[...] = suf
    m_prev = m_scr[...]
    m_new = m_prev
    for s in scores:
        m_new = jnp.maximum(m_new, jnp.max(s, axis=-1, keepdims=True))
    corr = jnp.exp(m_prev - m_new)
    l_new = corr * l_scr[...]
    pv = jnp.zeros((nh, HEAD_DIM), F32)
    for i in range(npg):
        p = jnp.exp(scores[i] - m_new)
        l_new = l_new + jnp.sum(p, axis=-1, keepdims=True)
        pb = p.astype(BF16)
        for h in range(nh):
            oh = lax.dot_general(pb, v_refs[i][0, 0, h].astype(BF16), (((1,), (1,)), ((), ())),
                                 preferred_element_type=F32)
            pv = pv + jnp.where(head == h, oh, 0.0)
    l_scr[...] = l_new
    acc_scr[...] = corr * acc_scr[...] + pv
    m_scr[...] = m_new

    @pl.when(step == pl.num_programs(1) - 1)
    def _():
        o_ref[0] = (acc_scr[...] / l_scr[...]).astype(o_ref.dtype)


def _fox_paged(q, k_new, v_new, lf_new, cache_kt, cache_vt, cache_lft, layer, page_table):
    rows, n_pages = page_table.shape
    ps = cache_lft.shape[3]
    npg = PAGES_PER_STEP
    assert n_pages % npg == 0
    nh, hd = N_HEADS, HEAD_DIM
    heads = lambda x: x.reshape(rows, nh, hd)

    def page(i, nd):
        return lambda b, p, pt: (layer, pt[b, n_pages - 1 - (p * npg + i)]) + (0,) * nd

    vec = pl.BlockSpec((1, nh, hd), lambda b, p, pt: (b, 0, 0))
    grid_spec = pltpu.PrefetchScalarGridSpec(
        num_scalar_prefetch=1,
        grid=(rows, n_pages // npg),
        in_specs=([vec, vec, vec, pl.BlockSpec((1, nh, 1), lambda b, p, pt: (b, 0, 0))]
                  + [pl.BlockSpec((1, 1, nh, hd, ps), page(i, 3)) for i in range(npg)]
                  + [pl.BlockSpec((1, 1, nh, hd, ps), page(i, 3)) for i in range(npg)]
                  + [pl.BlockSpec((1, 1, nh, ps), page(i, 2)) for i in range(npg)]),
        out_specs=vec,
        scratch_shapes=[pltpu.VMEM((nh, 1), F32), pltpu.VMEM((nh, 1), F32),
                        pltpu.VMEM((nh, hd), F32), pltpu.VMEM((nh, 1), F32)],
    )
    out = pl.pallas_call(
        _fox_paged_kernel,
        grid_spec=grid_spec,
        out_shape=jax.ShapeDtypeStruct((rows, nh, hd), BF16),
        compiler_params=_cparams(("arbitrary", "arbitrary")),
        name="fox_paged",
    )(page_table, heads(q), heads(k_new), heads(v_new), lf_new.reshape(rows, nh, 1),
      *([cache_kt] * npg), *([cache_vt] * npg), *([cache_lft] * npg))
    return out.reshape(rows, nh * hd)


def _forget_sample_kernel(zf_ref, bf_ref, lf_ref):
    lf_ref[...] = _log_sigmoid(zf_ref[...][:, :N_HEADS] + bf_ref[...])


def _forget_sample(z_fox, bf):
    rows = z_fox.shape[0]
    return pl.pallas_call(
        _forget_sample_kernel,
        grid=(1,),
        in_specs=[pl.BlockSpec((rows, LANE), lambda i: (0, 3 * W_BR // LANE)),
                  pl.BlockSpec((1, N_HEADS), lambda i: (0, 0))],
        out_specs=pl.BlockSpec((rows, N_HEADS), lambda i: (0, 0)),
        out_shape=jax.ShapeDtypeStruct((rows, N_HEADS), F32),
        name="forget_sample",
    )(z_fox, bf.reshape(1, N_HEADS))


CAST_CHUNKS = 4


def _cast_up_kernel(*refs, n_valid):
    w_refs, o_ref = refs[:-1], refs[-1]
    c = pl.program_id(2)
    for i, w_ref in enumerate(w_refs):
        valid = c * CAST_CHUNKS + i < n_valid
        o_ref[0, :, i * LANE:(i + 1) * LANE] = jnp.where(valid, w_ref[0], 0.0).astype(BF16)


def _cast_ffn_up(w_ffn_up, fp):
    depth, nj, d, f2 = w_ffn_up.shape
    f = f2 // 2
    assert f % LANE == 0 and fp % (CAST_CHUNKS * LANE) == 0
    nv, nc = f // LANE, fp // LANE // CAST_CHUNKS
    w = w_ffn_up.reshape(depth * nj, d, f2)

    def chunk(i):
        return lambda n, h, c: (n, 0, h * nv + jnp.minimum(c * CAST_CHUNKS + i, nv - 1))

    return pl.pallas_call(
        functools.partial(_cast_up_kernel, n_valid=nv),
        grid=(depth * nj, 2, nc),
        in_specs=[pl.BlockSpec((1, d, LANE), chunk(i)) for i in range(CAST_CHUNKS)],
        out_specs=pl.BlockSpec((1, d, CAST_CHUNKS * LANE), lambda n, h, c: (n, 0, h * nc + c)),
        out_shape=jax.ShapeDtypeStruct((depth * nj, d, 2 * fp), BF16),
        compiler_params=_cparams(("arbitrary", "arbitrary", "arbitrary")),
        name="cast_ffn_up",
    )(*([w] * CAST_CHUNKS))


def _cast_down_kernel(w_ref, o_ref, *, n_rows):
    tr = w_ref.shape[1]
    row = pl.program_id(1) * tr + lax.broadcasted_iota(jnp.int32, w_ref.shape[1:], 0)
    o_ref[0] = jnp.where(row < n_rows, w_ref[0], 0.0).astype(BF16)


def _cast_ffn_down(w_ffn_down, fp, tr):
    depth, nj, f, d = w_ffn_down.shape
    w = w_ffn_down.reshape(depth * nj, f, d)
    return pl.pallas_call(
        functools.partial(_cast_down_kernel, n_rows=f),
        grid=(depth * nj, fp // tr),
        in_specs=[pl.BlockSpec((1, tr, d), lambda i, r: (i, r, 0))],
        out_specs=pl.BlockSpec((1, tr, d), lambda i, r: (i, r, 0)),
        out_shape=jax.ShapeDtypeStruct((depth * nj, fp, d), BF16),
        compiler_params=_cparams(("arbitrary", "arbitrary")),
        name="cast_ffn_down",
    )(w)


def _prep_layer_weights(w_in, w_up_rw, w_up_fox, w_out):
    d = w_in.shape[0]
    w_rw = w_in[:, :RW_COLS].astype(BF16)
    w_fox = jnp.pad(w_in[:, RW_COLS:RW_COLS + FOX_COLS], ((0, 0), (0, FOX_PAD - FOX_COLS))).astype(BF16)
    w_gate = w_in[:, RW_COLS + FOX_COLS:].astype(BF16)
    assert w_gate.shape[1] == 2 * d
    return dict(rw=w_rw, fox=w_fox, gate=w_gate, up_rw=w_up_rw.astype(BF16),
                up_fox=w_up_fox.astype(BF16), out=w_out.astype(BF16))


def _layer(x, mods, lw, ffn_w, P, l, rows_per_batch, alpha, mixer_fn):
    m, d = x.shape
    seq = m if rows_per_batch == 1 else rows_per_batch
    tm_ffn = _tile(seq, 512)
    tm = _tile(seq, 1024)
    tf = FFN_TF
    mod = lambda i: _mod_arrays(mods, i, rows_per_batch)
    x = _ffn(x, mod(0), mod(1), mod(2), *ffn_w, 2 * l, P["ln_g"][l, 0], P["ln_b"][l, 0],
             rows_per_batch, alpha, tm_ffn, tf)
    sh, sc = mod(3), mod(4)
    z_rw = _proj(x, sh, sc, lw["rw"], rows_per_batch, tm, RW_COLS // 2)
    z_fox = _proj(x, sh, sc, lw["fox"], rows_per_batch, tm, FOX_PAD // 5)
    gates = _proj(x, sh, sc, lw["gate"], rows_per_batch, tm, _tile(d, 1024, LANE), act="sigmoid")
    o_rw, o_fox, st = mixer_fn(z_rw, z_fox)
    x = _merge(x, mod(5), o_rw, o_fox, gates, lw["up_rw"], lw["up_fox"], lw["out"],
               P["ln_g"][l, 1], P["ln_b"][l, 1], rows_per_batch, alpha, _tile(seq, 256))
    x = _ffn(x, mod(6), mod(7), mod(8), *ffn_w, 2 * l + 1, P["ln_g"][l, 2], P["ln_b"][l, 2],
             rows_per_batch, alpha, tm_ffn, tf)
    return x, st


def _mixer_prompt(z_rw, z_fox, P, l, batch):
    m = z_rw.shape[0]
    t = m // batch
    rw = _rw_prep_prompt(z_rw, batch, P["rw_mu"][l], P["rw_w0"][l], P["rw_w2"][l].astype(BF16),
                         P["rw_a0"][l], P["rw_a2"][l].astype(BF16), P["rw_g2"][l].astype(BF16), _tile(t, 256))
    o_rw, s_fin = _rwkv_prompt_scan(*rw, P["rw_kk"][l], P["rw_ka"][l], P["rw_rk"][l].reshape(-1),
                                    P["rw_lnw"][l], P["rw_lnb"][l], batch, _tile(t, 512, CHUNK), 8)
    q_aug, k_aug, v_hm, lf, k_t, v_t = _fox_prep(z_fox, batch, P["fox_bf"][l], _tile(t, 256))
    o_fox = _fox_prompt(q_aug, k_aug, v_hm, batch, _tile(t, 512))
    shift = z_rw.reshape(batch, t, RW_COLS)[:, -1]
    back = lambda a: jnp.transpose(a, (0, 3, 1, 2))
    st = (back(k_t), back(v_t), lf.reshape(batch, t, N_HEADS), s_fin, shift)
    return o_rw, o_fox, st


def _mixer_sample(z_rw, z_fox, P, l, shift0, state0, caches, page_table):
    rows = z_rw.shape[0]
    rw = _rw_prep_sample(z_rw, shift0, P["rw_mu"][l], P["rw_w0"][l], P["rw_w2"][l].astype(BF16),
                         P["rw_a0"][l], P["rw_a2"][l].astype(BF16), P["rw_g2"][l].astype(BF16))
    o_rw, s_new = _rwkv_sample_step(state0, *rw, P["rw_kk"][l], P["rw_ka"][l], P["rw_rk"][l].reshape(-1),
                                    P["rw_lnw"][l], P["rw_lnb"][l])
    lf = _forget_sample(z_fox, P["fox_bf"][l])
    k_new = z_fox[:, W_BR:2 * W_BR]
    v_new = z_fox[:, 2 * W_BR:3 * W_BR]
    o_fox = _fox_paged(z_fox[:, :W_BR], k_new, v_new, lf, *caches, l, page_table)
    heads = lambda a: a.reshape(rows, 1, N_HEADS, HEAD_DIM)
    st = (heads(k_new), heads(v_new), lf.reshape(rows, 1, N_HEADS), s_new, z_rw)
    return o_rw.astype(BF16), o_fox, st


def kernel(x_prompt, x_sample, c_prompt, c_sample, cache_k, cache_v, cache_logf, state_rwkv, state_shift, page_table, w_ada, b_ada, ln_g, ln_b, w_ffn_up, w_ffn_down, w_in, rw_mu, rw_w0, rw_w2, rw_a0, rw_a2, rw_g2, rw_kk, rw_ka, rw_rk, rw_lnw, rw_lnb, fox_bf, w_up_rw, w_up_fox, w_out):
    P = dict(ln_g=ln_g, ln_b=ln_b, rw_mu=rw_mu, rw_w0=rw_w0, rw_w2=rw_w2, rw_a0=rw_a0, rw_a2=rw_a2,
             rw_g2=rw_g2, rw_kk=rw_kk, rw_ka=rw_ka, rw_rk=rw_rk, rw_lnw=rw_lnw, rw_lnb=rw_lnb, fox_bf=fox_bf)
    depth = w_ada.shape[0]
    bp, t, d = x_prompt.shape
    bs, ts, _ = x_sample.shape
    assert ts == 1
    alpha = (2 * depth) ** 0.25
    rows = 16
    c_all = jnp.zeros((rows, d), F32).at[:bp].set(c_prompt).at[bp:bp + bs].set(c_sample)
    mods = _ada_mods(c_all, w_ada, b_ada).reshape(depth, rows, N_MOD, d)
    xp = x_prompt.reshape(bp * t, d)
    xs = x_sample.reshape(bs, d)
    caches = (jnp.transpose(cache_k, (0, 1, 3, 4, 2)), jnp.transpose(cache_v, (0, 1, 3, 4, 2)),
              jnp.transpose(cache_logf, (0, 1, 3, 2)))
    d_ff = w_ffn_down.shape[2]
    fp = -(-d_ff // FFN_TF) * FFN_TF
    ffn_w = (_cast_ffn_up(w_ffn_up, fp), _cast_ffn_down(w_ffn_down, fp, FFN_TF))
    outs_p, outs_s = [], []
    for l in range(depth):
        lw = _prep_layer_weights(w_in[l], w_up_rw[l], w_up_fox[l], w_out[l])
        xp, st = _layer(xp, mods[l, :bp], lw, ffn_w, P, l, t, alpha,
                        functools.partial(_mixer_prompt, P=P, l=l, batch=bp))
        outs_p.append(st)
        xs, st = _layer(xs, mods[l, bp:bp + bs], lw, ffn_w, P, l, 1, alpha,
                        functools.partial(_mixer_sample, P=P, l=l, shift0=state_shift[l], state0=state_rwkv[l],
                                          caches=caches, page_table=page_table))
        outs_s.append(st)
    stack = lambda outs, i: jnp.stack([o[i] for o in outs])
    return (xp.reshape(bp, t, d), xs.reshape(bs, 1, d),
            *(stack(outs_p, i) for i in range(5)), *(stack(outs_s, i) for i in range(5)))
```
